```python
import math
import jax, jax.numpy as jnp
from jax import lax
import numpy as np

D_MODEL = 1024
BATCH = 16
SEQ = 2048
DEPTH = 4

N_MIXERS = 2
RMS_EPS = 1e-6
LN_EPS = 1e-5
S5_WIDTH = D_MODEL
S5_GROUP = 16
S5_GROUPS = S5_WIDTH // S5_GROUP
S5_STATE = 64
S5_DT_MIN = 1e-3
S5_DT_MAX = 1e-1
ML_WIDTH = 2 * D_MODEL
ML_HEADS = 4
ML_HEAD_DIM = ML_WIDTH // ML_HEADS
ML_QKV_BLOCK = 4
ML_CONV = 5
ML_CHUNK = 128
N_S5_LAYERS = (DEPTH + 1) // 2
N_ML_LAYERS = DEPTH // 2

kernel_name = "bidir_s5_mlstm_hybrid_trunk"


def rmsnorm(x, g):
    xf = x.astype(jnp.float32)
    y = xf * lax.rsqrt(jnp.mean(xf * xf, axis=-1, keepdims=True) + RMS_EPS)
    return (y * g.astype(jnp.float32)).astype(x.dtype)


def ada_modulation(c, w, b):
    mod = jax.nn.silu(c) @ w + b
    shift, scale, gate = jnp.split(mod, 3, axis=-1)
    return shift[:, None, :], scale[:, None, :], gate[:, None, :]


def s5_direction(u, lam_re, lam_im, log_dt, b_re, b_im, c_re, c_im, reverse):
    L = u.shape[1]
    dt = jnp.exp(log_dt)[:, None]
    mag = jnp.exp(lam_re * dt)
    a_re = mag * jnp.cos(lam_im * dt)
    a_im = mag * jnp.sin(lam_im * dt)
    den = lam_re * lam_re + lam_im * lam_im
    z_re = ((a_re - 1.0) * lam_re + a_im * lam_im) / den
    z_im = (a_im * lam_re - (a_re - 1.0) * lam_im) / den
    bb_re = z_re[..., None] * b_re - z_im[..., None] * b_im
    bb_im = z_re[..., None] * b_im + z_im[..., None] * b_re
    bu_re = jnp.einsum('blgp,gnp->blgn', u, bb_re)
    bu_im = jnp.einsum('blgp,gnp->blgn', u, bb_im)
    ar = jnp.broadcast_to(a_re[None, None], (1, L) + a_re.shape)
    ai = jnp.broadcast_to(a_im[None, None], (1, L) + a_im.shape)

    def combine(e1, e2):
        a1r, a1i, b1r, b1i = e1
        a2r, a2i, b2r, b2i = e2
        return (a2r * a1r - a2i * a1i,
                a2r * a1i + a2i * a1r,
                a2r * b1r - a2i * b1i + b2r,
                a2r * b1i + a2i * b1r + b2i)

    _, _, s_re, s_im = lax.associative_scan(combine, (ar, ai, bu_re, bu_im), axis=1, reverse=reverse)
    return jnp.einsum('blgn,gpn->blgp', s_re, c_re) - jnp.einsum('blgn,gpn->blgp', s_im, c_im)


def s5_mixer(h, w_in, lam_re, lam_im, log_dt, b_re, b_im, c_re, c_im, d_skip, w_glu, b_glu, w_out):
    Bsz, L, _ = h.shape
    u, z = jnp.split(h @ w_in, 2, axis=-1)
    ug = u.reshape(Bsz, L, S5_GROUPS, S5_GROUP)
    y_f = s5_direction(ug, lam_re[0], lam_im[0], log_dt[0], b_re[0], b_im[0], c_re[0], c_im[0], False)
    y_b = s5_direction(ug, lam_re[1], lam_im[1], log_dt[1], b_re[1], b_im[1], c_re[1], c_im[1], True)
    y = (y_f + y_b).reshape(Bsz, L, S5_WIDTH) + d_skip * u
    y = jax.nn.gelu(y)
    y = y * jax.nn.sigmoid(y @ w_glu + b_glu)
    return (y * jax.nn.silu(z)) @ w_out


def depthwise_conv_centred(x, w, b):
    k = w.shape[0]
    y = lax.conv_general_dilated(x, w[:, None, :], window_strides=(1,),
                                 padding=[(k // 2, k // 2)],
                                 dimension_numbers=('NWC', 'WIO', 'NWC'),
                                 feature_group_count=x.shape[-1])
    return y + b


def headwise_linear(x, w):
    Bsz, L, E = x.shape
    xb = x.reshape(Bsz, L, E // ML_QKV_BLOCK, ML_QKV_BLOCK)
    return jnp.einsum('blnj,njk->blnk', xb, w).reshape(Bsz, L, E)


def mlstm_chunkwise(q, k, v, i_pre, f_pre):
    Bsz, H, L, d = q.shape
    T = ML_CHUNK
    nc = L // T
    q = q.astype(jnp.float32)
    k = k.astype(jnp.float32) * (d ** -0.5)
    v = v.astype(jnp.float32)
    li = i_pre.astype(jnp.float32)
    lf = jax.nn.log_sigmoid(f_pre.astype(jnp.float32))

    def chunks(t):
        return jnp.moveaxis(t.reshape((Bsz, H, nc, T) + t.shape[3:]), 2, 0)

    xs = (chunks(q), chunks(k), chunks(v), chunks(li), chunks(lf))
    lower = jnp.tril(jnp.ones((T, T), dtype=bool))

    def step(carry, inp):
        C, n, m = carry
        qt, kt, vt, it, ft = inp
        b = jnp.cumsum(ft, axis=-1)
        dmat = b[..., :, None] - b[..., None, :] + it[..., None, :]
        dmat = jnp.where(lower, dmat, -jnp.inf)
        inter = b + m[..., None]
        m_t = jnp.maximum(inter, jnp.max(dmat, axis=-1))
        w_intra = jnp.exp(dmat - m_t[..., None])
        w_inter = jnp.exp(inter - m_t)
        s = jnp.einsum('bhtd,bhsd->bhts', qt, kt) * w_intra
        num = jnp.einsum('bhts,bhsd->bhtd', s, vt) + w_inter[..., None] * jnp.einsum('bhtd,bhde->bhte', qt, C)
        nq = jnp.sum(s, axis=-1) + w_inter * jnp.einsum('bhtd,bhd->bht', qt, n)
        h = num / jnp.maximum(jnp.abs(nq), jnp.exp(-m_t))[..., None]
        bT = b[..., -1]
        g = bT[..., None] - b + it
        m_new = jnp.maximum(bT + m, jnp.max(g, axis=-1))
        wk = jnp.exp(g - m_new[..., None])
        decay = jnp.exp(bT + m - m_new)
        kw = kt * wk[..., None]
        C_new = decay[..., None, None] * C + jnp.einsum('bhsd,bhse->bhde', kw, vt)
        n_new = decay[..., None] * n + jnp.sum(kw, axis=2)
        return (C_new, n_new, m_new), h

    init = (jnp.zeros((Bsz, H, d, d), jnp.float32),
            jnp.zeros((Bsz, H, d), jnp.float32),
            jnp.zeros((Bsz, H), jnp.float32))
    _, hs = lax.scan(step, init, xs)
    return jnp.moveaxis(hs, 0, 2).reshape(Bsz, H, L, d)


def mlstm_mixer(h, w_in, conv_w, conv_b, w_q, w_k, w_v, w_gates, b_gates, gn_w, skip, w_out):
    Bsz, L, _ = h.shape
    xm, z = jnp.split(h @ w_in, 2, axis=-1)
    xc = jax.nn.silu(depthwise_conv_centred(xm, conv_w, conv_b))
    q = headwise_linear(xc, w_q)
    k = headwise_linear(xc, w_k)
    v = headwise_linear(xm, w_v)
    gates = (jnp.einsum('ble,eg->blg', q, w_gates[0]) + jnp.einsum('ble,eg->blg', k, w_gates[1])
             + jnp.einsum('ble,eg->blg', v, w_gates[2]) + b_gates).astype(jnp.float32)
    gates = gates.reshape(Bsz, L, 2, 2, ML_HEADS).transpose(2, 3, 0, 4, 1)

    def heads(t):
        return t.reshape(Bsz, L, ML_HEADS, ML_HEAD_DIM).transpose(0, 2, 1, 3)

    def flip(t):
        return jnp.flip(t, axis=2)

    qh, kh, vh = heads(q), heads(k), heads(v)
    h_f = mlstm_chunkwise(qh, kh, vh, gates[0, 0], gates[0, 1])
    h_b = flip(mlstm_chunkwise(flip(qh), flip(kh), flip(vh), flip(gates[1, 0]), flip(gates[1, 1])))
    hs = h_f + h_b
    mu = jnp.mean(hs, axis=-1, keepdims=True)
    var = jnp.mean(jnp.square(hs - mu), axis=-1, keepdims=True)
    hn = (hs - mu) * lax.rsqrt(var + LN_EPS)
    hn = hn.transpose(0, 2, 1, 3).reshape(Bsz, L, ML_WIDTH).astype(h.dtype) * gn_w
    y = (hn + skip * xc) * jax.nn.silu(z)
    return y @ w_out


def setup_inputs(seed: int = 0) -> dict:
    key = jax.random.key(seed)
    ks = iter(jax.random.split(key, 40))
    f32 = jnp.float32

    def nrm(shape, scale):
        return jax.random.normal(next(ks), shape, f32) * scale

    D, EA, EM, G, N, P, H = D_MODEL, S5_WIDTH, ML_WIDTH, S5_GROUPS, S5_STATE, S5_GROUP, ML_HEADS
    nA, nB = N_S5_LAYERS, N_ML_LAYERS
    inp = {}
    inp['x'] = nrm((BATCH, SEQ, D), 1.0)
    inp['c'] = nrm((BATCH, D), 1.0)
    inp['ada_w'] = nrm((DEPTH, D, 3 * D), 0.5 * D ** -0.5)
    inp['ada_b'] = nrm((DEPTH, 3 * D), 0.01)
    inp['norm_g'] = 1.0 + nrm((DEPTH, D), 0.02)
    inp['s5_w_in'] = nrm((nA, D, 2 * EA), D ** -0.5)
    n_idx = jnp.arange(N, dtype=f32)
    inp['s5_lam_re'] = -0.5 + nrm((nA, 2, G, N), 0.01)
    inp['s5_lam_im'] = math.pi * n_idx + nrm((nA, 2, G, N), 0.01)
    inp['s5_log_dt'] = jax.random.uniform(next(ks), (nA, 2, G), f32,
                                          math.log(S5_DT_MIN), math.log(S5_DT_MAX))
    inp['s5_b_re'] = nrm((nA, 2, G, N, P), (2.0 * P) ** -0.5)
    inp['s5_b_im'] = nrm((nA, 2, G, N, P), (2.0 * P) ** -0.5)
    inp['s5_c_re'] = nrm((nA, 2, G, P, N), (2.0 * N) ** -0.5)
    inp['s5_c_im'] = nrm((nA, 2, G, P, N), (2.0 * N) ** -0.5)
    inp['s5_d'] = nrm((nA, EA), 0.5)
    inp['s5_w_glu'] = nrm((nA, EA, EA), EA ** -0.5)
    inp['s5_b_glu'] = nrm((nA, EA), 0.01)
    inp['s5_w_out'] = nrm((nA, EA, D), EA ** -0.5)
    inp['ml_w_in'] = nrm((nB, D, 2 * EM), D ** -0.5)
    inp['ml_conv_w'] = nrm((nB, ML_CONV, EM), ML_CONV ** -0.5)
    inp['ml_conv_b'] = nrm((nB, EM), 0.01)
    nblk = EM // ML_QKV_BLOCK
    inp['ml_w_q'] = nrm((nB, nblk, ML_QKV_BLOCK, ML_QKV_BLOCK), ML_QKV_BLOCK ** -0.5)
    inp['ml_w_k'] = nrm((nB, nblk, ML_QKV_BLOCK, ML_QKV_BLOCK), ML_QKV_BLOCK ** -0.5)
    inp['ml_w_v'] = nrm((nB, nblk, ML_QKV_BLOCK, ML_QKV_BLOCK), ML_QKV_BLOCK ** -0.5)
    inp['ml_w_gates'] = nrm((nB, 3, EM, 4 * H), 0.1 * (3.0 * EM) ** -0.5)
    f_bias = jnp.linspace(3.0, 6.0, H, dtype=f32)
    i_bias = nrm((nB, 2, 1, H), 0.1)
    fb = f_bias + nrm((nB, 2, 1, H), 0.1)
    inp['ml_b_gates'] = jnp.concatenate([i_bias, fb], axis=2).reshape(nB, 4 * H)
    inp['ml_gn_w'] = 1.0 + nrm((nB, EM), 0.02)
    inp['ml_skip'] = 1.0 + nrm((nB, EM), 0.02)
    inp['ml_w_out'] = nrm((nB, EM, D), EM ** -0.5)
    inp['final_g'] = 1.0 + nrm((D,), 0.02)
    return inp


def reference(x, c, ada_w, ada_b, norm_g,
              s5_w_in, s5_lam_re, s5_lam_im, s5_log_dt, s5_b_re, s5_b_im, s5_c_re, s5_c_im,
              s5_d, s5_w_glu, s5_b_glu, s5_w_out,
              ml_w_in, ml_conv_w, ml_conv_b, ml_w_q, ml_w_k, ml_w_v, ml_w_gates, ml_b_gates,
              ml_gn_w, ml_skip, ml_w_out, final_g):
    for i in range(DEPTH):
        shift, scale, gate = ada_modulation(c, ada_w[i], ada_b[i])
        h = rmsnorm(x, norm_g[i]) * (1.0 + scale) + shift
        j = i // N_MIXERS
        if i % N_MIXERS == 0:
            y = s5_mixer(h, s5_w_in[j], s5_lam_re[j], s5_lam_im[j], s5_log_dt[j],
                         s5_b_re[j], s5_b_im[j], s5_c_re[j], s5_c_im[j],
                         s5_d[j], s5_w_glu[j], s5_b_glu[j], s5_w_out[j])
        else:
            y = mlstm_mixer(h, ml_w_in[j], ml_conv_w[j], ml_conv_b[j], ml_w_q[j], ml_w_k[j], ml_w_v[j],
                            ml_w_gates[j], ml_b_gates[j], ml_gn_w[j], ml_skip[j], ml_w_out[j])
        x = x + gate * y
    return rmsnorm(x, final_g)
```

```python
import functools

import jax
import jax.numpy as jnp
from jax import lax
from jax.experimental import pallas as pl
from jax.experimental.pallas import tpu as pltpu

F32 = jnp.float32
BF16 = jnp.bfloat16
HIGHEST = lax.Precision.HIGHEST

RMS_EPS = 1e-6
LN_EPS = 1e-5
CHUNK = 128
V7X_LANES = 128
V7X_MXU_DIM = 256
VMEM_LIMIT_BYTES = 56 * 1024 * 1024
TOKEN_TILE = 512


def _params(*sem):
    return pltpu.CompilerParams(dimension_semantics=sem, vmem_limit_bytes=VMEM_LIMIT_BYTES)


def _sds(shape, dtype):
    return jax.ShapeDtypeStruct(shape, dtype)


def _silu(v):
    return v * jax.nn.sigmoid(v)


def _nt_dot(a, b):
    return lax.dot_general(a, b, (((1,), (1,)), ((), ())), preferred_element_type=F32)


def _modulated_norm(x, g, shift, scale):
    ms = jnp.mean(x * x, axis=-1, keepdims=True)
    h = x * lax.rsqrt(ms + RMS_EPS) * g
    return h * (1.0 + scale) + shift


def _ada_kernel(c_ref, w_ref, b_ref, o_ref):
    sc = _silu(c_ref[...])
    o_ref[0] = jnp.dot(sc, w_ref[0], preferred_element_type=F32, precision=HIGHEST) + b_ref[0]


def _ada(c, ada_w, ada_b):
    depth, d, d3 = ada_w.shape
    b = c.shape[0]
    return pl.pallas_call(
        _ada_kernel,
        grid=(depth, d3 // d),
        in_specs=[pl.BlockSpec((b, d), lambda i, j: (0, 0)),
                  pl.BlockSpec((1, d, d), lambda i, j: (i, 0, j)),
                  pl.BlockSpec((1, 1, d), lambda i, j: (i, 0, j))],
        out_specs=pl.BlockSpec((1, b, d), lambda i, j: (i, 0, j)),
        out_shape=_sds((depth, b, d3), F32),
        compiler_params=_params("arbitrary", "arbitrary"),
        name="ada_mod",
    )(c, ada_w, ada_b.reshape(depth, 1, d3))


def _s5_pre_kernel(x_ref, shift_ref, scale_ref, g_ref, wt_ref, ut_ref, zt_ref, *, e, t):
    h = _modulated_norm(x_ref[0], g_ref[...], shift_ref[0], scale_ref[0])
    uz = _nt_dot(wt_ref[...], h.astype(BF16))
    for k in range(h.shape[0] // t):
        ut_ref[0, k] = uz[:e, k * t:(k + 1) * t]
    zt_ref[0] = uz[e:].astype(BF16)


def _s5_pre(x, shift, scale, g, w_in_t):
    b, l, d = x.shape
    e = w_in_t.shape[0] // 2
    tl = min(TOKEN_TILE, l)
    t = CHUNK
    return pl.pallas_call(
        functools.partial(_s5_pre_kernel, e=e, t=t),
        grid=(b, l // tl),
        in_specs=[pl.BlockSpec((1, tl, d), lambda i, j: (i, j, 0)),
                  pl.BlockSpec((1, 1, d), lambda i, j: (i, 0, 0)),
                  pl.BlockSpec((1, 1, d), lambda i, j: (i, 0, 0)),
                  pl.BlockSpec((1, d), lambda i, j: (0, 0)),
                  pl.BlockSpec((2 * e, d), lambda i, j: (0, 0))],
        out_specs=[pl.BlockSpec((1, tl // t, e, t), lambda i, j: (i, j, 0, 0)),
                   pl.BlockSpec((1, e, tl), lambda i, j: (i, 0, j))],
        out_shape=[_sds((b, l // t, e, t), F32), _sds((b, e, l), BF16)],
        compiler_params=_params("arbitrary", "arbitrary"),
        name="s5_pre",
    )(x, shift, scale, g, w_in_t)


def _abar(lre, lim, ldt):
    dt = jnp.exp(ldt)
    mag = jnp.exp(lre * dt)
    are = mag * jnp.cos(lim * dt)
    aim = mag * jnp.sin(lim * dt)
    den = lre * lre + lim * lim
    zre = ((are - 1.0) * lre + aim * lim) / den
    zim = (aim * lre - (are - 1.0) * lim) / den
    return are, aim, zre, zim


def _cplx_pow(ar, ai, expo, nbits):
    shape = expo.shape
    rr = jnp.ones(shape, F32)
    ri = jnp.zeros(shape, F32)
    pr = jnp.broadcast_to(ar, shape)
    pi = jnp.broadcast_to(ai, shape)
    for k in range(nbits):
        bit = ((expo >> k) & 1) == 1
        mr = jnp.where(bit, pr, 1.0)
        mi = jnp.where(bit, pi, 0.0)
        rr, ri = rr * mr - ri * mi, rr * mi + ri * mr
        if k + 1 < nbits:
            pr, pi = pr * pr - pi * pi, 2.0 * pr * pi
    return rr, ri


def _s5_gen_kernel(lre_r, lim_r, lre_c, lim_c, ldt, btre, btim, cre, cim, ctre, ctim,
                   toep_ref, xmat_ref, ymat_ref, adec_ref, kf_scr, *, t, n, p):
    n2 = 2 * n
    nbits = t.bit_length()
    lane_row = lax.broadcasted_iota(jnp.int32, (1, n2), 1)
    lane_nt = lax.broadcasted_iota(jnp.int32, (n2, t), 1)
    sub_nt = lax.broadcasted_iota(jnp.int32, (n2, t), 0)
    sub_tn = lax.broadcasted_iota(jnp.int32, (t, n2), 0)
    first_half_row = lane_row < n
    first_half_sub = sub_nt < n
    lane_pp = lax.broadcasted_iota(jnp.int32, (p * p, n2), 1)
    lane_p = lax.broadcasted_iota(jnp.int32, (p, n2), 1)

    k_halves = []
    k0_bwd = None
    for d in range(2):
        ldt_d = ldt[d]
        are_r, aim_r, zre_r, zim_r = _abar(lre_r[d], lim_r[d], ldt_d)
        are_c, aim_c, _, _ = _abar(lre_c[d], lim_c[d], ldt_d)

        pr, pi = are_r, aim_r
        for _ in range(t.bit_length() - 1):
            pr, pi = pr * pr - pi * pi, 2.0 * pr * pi
        adec_ref[2 * d:2 * d + 1, :] = pr
        adec_ref[2 * d + 1:2 * d + 2, :] = jnp.where(first_half_row, -pi, pi)

        bbt_re = zre_r * btre[d] - zim_r * btim[d]
        bbt_im = zre_r * btim[d] + zim_r * btre[d]

        rows = []
        for pp in range(p):
            c_re = cre[d][pp:pp + 1, :]
            c_im = cim[d][pp:pp + 1, :]
            cb_re = c_re * bbt_re - c_im * bbt_im
            cb_im = c_re * bbt_im + c_im * bbt_re
            rows.append(jnp.where(lane_p < n, cb_re, -cb_im))
        cbw = jnp.concatenate(rows, axis=0)
        expo = lane_nt if d == 0 else t - lane_nt
        pw_re, pw_im = _cplx_pow(are_c, aim_c, expo, nbits)
        prhs = jnp.where(first_half_sub, pw_re, pw_im)
        k_halves.append(jnp.dot(cbw, prhs, preferred_element_type=F32, precision=HIGHEST))
        if d == 1:
            k0_bwd = jnp.sum(jnp.where(lane_pp < n, cbw, 0.0), axis=1, keepdims=True)

        expo_x = (t - 1 - sub_tn) if d == 0 else sub_tn
        pt_re, pt_im = _cplx_pow(are_r, aim_r, expo_x, nbits)
        for q in range(p):
            b_re = bbt_re[q:q + 1, :]
            b_im = bbt_im[q:q + 1, :]
            v1 = jnp.where(first_half_row, b_re, b_im)
            v2 = jnp.where(first_half_row, b_im, -b_re)
            xmat_ref[q * t:(q + 1) * t, d * n2:(d + 1) * n2] = (pt_re * v1 - pt_im * v2).astype(BF16)

        expo_y = (lane_nt + 1) if d == 0 else (t - lane_nt)
        p1_re, p1_im = _cplx_pow(are_c, aim_c, expo_y, nbits)
        ya = jnp.where(first_half_sub, p1_re, -p1_im)
        yb = jnp.where(first_half_sub, p1_im, p1_re)
        for pp in range(p):
            ymat_ref[d * n2:(d + 1) * n2, pp * t:(pp + 1) * t] = (
                ctre[d][:, pp:pp + 1] * ya - ctim[d][:, pp:pp + 1] * yb).astype(BF16)

    lane_k = lax.broadcasted_iota(jnp.int32, (p * p, t), 1)
    kf_scr[0] = k_halves[0] + jnp.where(lane_k == 0, k0_bwd, 0.0)
    kf_scr[1] = k_halves[1]

    def q_body(q, carry):
        for pp in range(p):
            r = pp * p + q
            row = jnp.concatenate([kf_scr[0, pl.ds(r, 1), :], kf_scr[1, pl.ds(r, 1), :]], axis=1)
            full = jnp.broadcast_to(row, (t, 2 * t))
            rolled = pltpu.roll(full, 0, 1, stride=1, stride_axis=0)
            toep_ref[pl.ds(pl.multiple_of(q * t, t), t), pp * t:(pp + 1) * t] = rolled[:, :t].astype(BF16)
        return carry

    lax.fori_loop(0, p, q_body, 0)


def _s5_gen(lam_re, lam_im, log_dt, b_re, b_im, c_re, c_im):
    _, g, n, p = b_re.shape
    t = CHUNK
    n2 = 2 * n
    dup_l = lambda a: jnp.concatenate([a, a], axis=-1)
    dup_s = lambda a: jnp.concatenate([a, a], axis=-2)
    lre_r = dup_l(lam_re)[:, :, None, :]
    lim_r = dup_l(lam_im)[:, :, None, :]
    lre_c = dup_s(lam_re[..., None])
    lim_c = dup_s(lam_im[..., None])
    ldt = log_dt[:, :, None, None]
    btre = dup_l(jnp.swapaxes(b_re, -1, -2))
    btim = dup_l(jnp.swapaxes(b_im, -1, -2))
    cre = dup_l(c_re)
    cim = dup_l(c_im)
    ctre = dup_s(jnp.swapaxes(c_re, -1, -2))
    ctim = dup_s(jnp.swapaxes(c_im, -1, -2))

    def per_group(a):
        return pl.BlockSpec((2, None) + a.shape[2:], lambda i: (0, i, 0, 0))

    ins = (lre_r, lim_r, lre_c, lim_c, ldt, btre, btim, cre, cim, ctre, ctim)
    pt = p * t
    return pl.pallas_call(
        functools.partial(_s5_gen_kernel, t=t, n=n, p=p),
        grid=(g,),
        in_specs=[per_group(a) for a in ins],
        out_specs=[pl.BlockSpec((None, pt, pt), lambda i: (i, 0, 0)),
                   pl.BlockSpec((None, pt, 2 * n2), lambda i: (i, 0, 0)),
                   pl.BlockSpec((None, 2 * n2, pt), lambda i: (i, 0, 0)),
                   pl.BlockSpec((None, 4, n2), lambda i: (i, 0, 0))],
        out_shape=[_sds((g, pt, pt), BF16), _sds((g, pt, 2 * n2), BF16),
                   _sds((g, 2 * n2, pt), BF16), _sds((g, 4, n2), F32)],
        scratch_shapes=[pltpu.VMEM((2, p * p, t), F32)],
        compiler_params=_params("arbitrary"),
        name="s5_gen",
    )(*ins)


def _s5_ssm_kernel(u_ref, toep_ref, xmat_ref, ymat_ref, adec_ref, y_ref,
                   lhs_scr, xs_scr, sin_scr, *, nc, p, t, n):
    n2 = 2 * n
    bsz = u_ref.shape[0] // nc
    for q in range(p):
        lhs_scr[:, q * t:(q + 1) * t] = u_ref[:, q, :].astype(BF16)
    lhs = lhs_scr[...]

    for d in range(2):
        xs_scr[d] = jnp.dot(lhs, xmat_ref[:, d * n2:(d + 1) * n2], preferred_element_type=F32)

    for d in range(2):
        a1 = adec_ref[2 * d:2 * d + 1, :]
        a2 = adec_ref[2 * d + 1:2 * d + 2, :]
        s = jnp.zeros((bsz, n2), F32)
        for c in (range(nc) if d == 0 else reversed(range(nc))):
            rows = pl.ds(c, bsz, stride=nc)
            sin_scr[d, rows, :] = s
            s = a1 * s + a2 * pltpu.roll(s, n, 1) + xs_scr[d, rows, :]

    sin_f = sin_scr[0].astype(BF16)
    sin_b = sin_scr[1].astype(BF16)
    w = 2 * t
    for pp in range(p * t // w):
        cols = slice(pp * w, (pp + 1) * w)
        acc = jnp.dot(lhs, toep_ref[:, cols], preferred_element_type=F32)
        acc = acc + jnp.dot(sin_f, ymat_ref[0:n2, cols], preferred_element_type=F32)
        acc = acc + jnp.dot(sin_b, ymat_ref[n2:2 * n2, cols], preferred_element_type=F32)
        for h in range(w // t):
            y_ref[:, pp * (w // t) + h, :] = acc[:, h * t:(h + 1) * t]


def _s5_ssm(ut, toep, xmat, ymat, adec):
    b, nc, e, t = ut.shape
    g, pt, _ = toep.shape
    p = pt // t
    n2 = adec.shape[-1]
    bc = b * nc
    u3 = ut.reshape(bc, e, t)
    y3 = pl.pallas_call(
        functools.partial(_s5_ssm_kernel, nc=nc, p=p, t=t, n=n2 // 2),
        grid=(g,),
        in_specs=[pl.BlockSpec((bc, p, t), lambda i: (0, i, 0)),
                  pl.BlockSpec((None, pt, pt), lambda i: (i, 0, 0)),
                  pl.BlockSpec((None, pt, 2 * n2), lambda i: (i, 0, 0)),
                  pl.BlockSpec((None, 2 * n2, pt), lambda i: (i, 0, 0)),
                  pl.BlockSpec((None, 4, n2), lambda i: (i, 0, 0))],
        out_specs=pl.BlockSpec((bc, p, t), lambda i: (0, i, 0)),
        out_shape=_sds((bc, e, t), F32),
        scratch_shapes=[pltpu.VMEM((bc, pt), BF16),
                        pltpu.VMEM((2, bc, n2), F32),
                        pltpu.VMEM((2, bc, n2), F32)],
        compiler_params=_params("arbitrary"),
        name="s5_ssm",
    )(u3, toep, xmat, ymat, adec)
    return y3


def _s5_post_kernel(y_ref, u_ref, zt_ref, x_ref, gate_ref, dcol_ref, bcol_ref, wgt_ref, wot_ref,
                    o_ref, *, k):
    cat = lambda parts: parts[0] if len(parts) == 1 else jnp.concatenate(parts, axis=1)
    yt = cat([y_ref[kk] for kk in range(k)])
    ut = cat([u_ref[kk] for kk in range(k)])
    reps = yt.shape[1] // dcol_ref.shape[1]
    dfull = cat([dcol_ref[...]] * reps)
    bfull = cat([bcol_ref[...]] * reps)
    y = jax.nn.gelu(yt + dfull * ut)
    g = jnp.dot(wgt_ref[...], y.astype(BF16), preferred_element_type=F32) + bfull
    y = y * jax.nn.sigmoid(g)
    yz = y * _silu(zt_ref[0].astype(F32))
    ot = jnp.dot(wot_ref[...], yz.astype(BF16), preferred_element_type=F32)
    o_ref[0] = x_ref[0] + gate_ref[0] * ot.T


def _s5_post(y3, ut, zt, x, gate, d_skip, b_glu, w_glu_t, w_out_t):
    b, l, d = x.shape
    _, nc, e, t = ut.shape
    tl = min(TOKEN_TILE, l)
    k = tl // t
    u3 = ut.reshape(b * nc, e, t)
    lanes = V7X_LANES
    dcol = jnp.broadcast_to(d_skip[:, None], (e, lanes))
    bcol = jnp.broadcast_to(b_glu[:, None], (e, lanes))
    steps = l // tl
    return pl.pallas_call(
        functools.partial(_s5_post_kernel, k=k),
        grid=(b, steps),
        in_specs=[pl.BlockSpec((k, e, t), lambda i, j: (i * steps + j, 0, 0)),
                  pl.BlockSpec((k, e, t), lambda i, j: (i * steps + j, 0, 0)),
                  pl.BlockSpec((1, e, tl), lambda i, j: (i, 0, j)),
                  pl.BlockSpec((1, tl, d), lambda i, j: (i, j, 0)),
                  pl.BlockSpec((1, 1, d), lambda i, j: (i, 0, 0)),
                  pl.BlockSpec((e, lanes), lambda i, j: (0, 0)),
                  pl.BlockSpec((e, lanes), lambda i, j: (0, 0)),
                  pl.BlockSpec((e, e), lambda i, j: (0, 0)),
                  pl.BlockSpec((d, e), lambda i, j: (0, 0))],
        out_specs=pl.BlockSpec((1, tl, d), lambda i, j: (i, j, 0)),
        out_shape=_sds((b, l, d), F32),
        compiler_params=_params("arbitrary", "arbitrary"),
        name="s5_post",
    )(y3, u3, zt, x, gate, dcol, bcol, w_glu_t, w_out_t)


def _s5_layer(x, shift, scale, gate, g, w_in, lam_re, lam_im, log_dt, b_re, b_im, c_re, c_im,
              d_skip, w_glu, b_glu, w_out):
    ut, zt = _s5_pre(x, shift, scale, g, w_in.T.astype(BF16))
    toep, xmat, ymat, adec = _s5_gen(lam_re, lam_im, log_dt, b_re, b_im, c_re, c_im)
    y3 = _s5_ssm(ut, toep, xmat, ymat, adec)
    return _s5_post(y3, ut, zt, x, gate, d_skip, b_glu, w_glu.T.astype(BF16), w_out.T.astype(BF16))


def _ml_pre_kernel(x_ref, shift_ref, scale_ref, g_ref, w_ref, xm_ref, z_ref, *, e):
    h = _modulated_norm(x_ref[0], g_ref[...], shift_ref[0], scale_ref[0])
    xz = jnp.dot(h.astype(BF16), w_ref[...], preferred_element_type=F32)
    xm_ref[0] = xz[:, :e]
    z_ref[0] = xz[:, e:].astype(BF16)


def _ml_pre(x, shift, scale, g, w_in):
    b, l, d = x.shape
    e = w_in.shape[1] // 2
    tl = min(TOKEN_TILE, l)
    return pl.pallas_call(
        functools.partial(_ml_pre_kernel, e=e),
        grid=(b, l // tl),
        in_specs=[pl.BlockSpec((1, tl, d), lambda i, j: (i, j, 0)),
                  pl.BlockSpec((1, 1, d), lambda i, j: (i, 0, 0)),
                  pl.BlockSpec((1, 1, d), lambda i, j: (i, 0, 0)),
                  pl.BlockSpec((1, d), lambda i, j: (0, 0)),
                  pl.BlockSpec((d, 2 * e), lambda i, j: (0, 0))],
        out_specs=[pl.BlockSpec((1, tl, e), lambda i, j: (i, j, 0)),
                   pl.BlockSpec((1, tl, e), lambda i, j: (i, j, 0))],
        out_shape=[_sds((b, l, e), F32), _sds((b, l, e), BF16)],
        compiler_params=_params("arbitrary", "arbitrary"),
        name="ml_pre",
    )(x, shift, scale, g, w_in)


HALO = 8


def _ml_qkv_kernel(xm_ref, prev_ref, next_ref, cw_ref, cb_ref, wq_ref, wk_ref, wv_ref, wg_ref, bg_ref,
                   q_ref, k_ref, v_ref, xc_ref, gp_ref, *, steps, taps):
    j = pl.program_id(1)
    xm = xm_ref[0]
    tq, e = xm.shape
    prev = jnp.where(j == 0, 0.0, prev_ref[0])
    nxt = jnp.where(j == steps - 1, 0.0, next_ref[0])
    xe = jnp.concatenate([prev, xm, nxt], axis=0)
    pad = taps // 2
    acc = jnp.broadcast_to(cb_ref[...], (tq, e))
    for kk in range(taps):
        off = HALO + kk - pad
        acc = acc + xe[off:off + tq, :] * cw_ref[kk:kk + 1, :]
    xc = _silu(acc)
    xc_b = xc.astype(BF16)
    xm_b = xm.astype(BF16)
    tile = wq_ref.shape[-1]
    qs, ks, vs = [], [], []
    for i in range(e // tile):
        cols = slice(i * tile, (i + 1) * tile)
        qs.append(jnp.dot(xc_b[:, cols], wq_ref[i], preferred_element_type=F32))
        ks.append(jnp.dot(xc_b[:, cols], wk_ref[i], preferred_element_type=F32))
        vs.append(jnp.dot(xm_b[:, cols], wv_ref[i], preferred_element_type=F32))
    q = jnp.concatenate(qs, axis=1).astype(BF16)
    k = jnp.concatenate(ks, axis=1).astype(BF16)
    v = jnp.concatenate(vs, axis=1).astype(BF16)
    gp = (jnp.dot(q, wg_ref[0], preferred_element_type=F32)
          + jnp.dot(k, wg_ref[1], preferred_element_type=F32)
          + jnp.dot(v, wg_ref[2], preferred_element_type=F32) + bg_ref[...])
    q_ref[0] = q
    k_ref[0] = k
    v_ref[0] = v
    xc_ref[0] = xc_b
    gp_ref[0] = gp


def _blockdiag_tiles(w, tile):
    nblk, bs, _ = w.shape
    per = tile // bs
    w4 = w.reshape(nblk // per, per, bs, bs)
    eye = jnp.eye(per, dtype=w.dtype)
    return jnp.einsum('tirc,ij->tirjc', w4, eye).reshape(nblk // per, tile, tile)


def _ml_qkv(xm, conv_w, conv_b, w_q, w_k, w_v, w_gates, b_gates):
    b, l, e = xm.shape
    tq = min(TOKEN_TILE, l)
    steps = l // tq
    taps = conv_w.shape[0]
    tile = V7X_MXU_DIM
    lanes = V7X_LANES
    ng = w_gates.shape[-1]
    wq = _blockdiag_tiles(w_q, tile).astype(BF16)
    wk = _blockdiag_tiles(w_k, tile).astype(BF16)
    wv = _blockdiag_tiles(w_v, tile).astype(BF16)
    wg = jnp.pad(w_gates, ((0, 0), (0, 0), (0, lanes - ng))).astype(BF16)
    bg = jnp.pad(b_gates, (0, lanes - ng)).reshape(1, lanes)
    hb = tq // HALO
    last = l // HALO - 1
    tok = lambda i, j: (i, j, 0)
    full3 = lambda i, j: (0, 0, 0)
    full2 = lambda i, j: (0, 0)
    return pl.pallas_call(
        functools.partial(_ml_qkv_kernel, steps=steps, taps=taps),
        grid=(b, steps),
        in_specs=[pl.BlockSpec((1, tq, e), tok),
                  pl.BlockSpec((1, HALO, e), lambda i, j: (i, jnp.maximum(j * hb - 1, 0), 0)),
                  pl.BlockSpec((1, HALO, e), lambda i, j: (i, jnp.minimum((j + 1) * hb, last), 0)),
                  pl.BlockSpec((taps, e), full2),
                  pl.BlockSpec((1, e), full2),
                  pl.BlockSpec(wq.shape, full3),
                  pl.BlockSpec(wk.shape, full3),
                  pl.BlockSpec(wv.shape, full3),
                  pl.BlockSpec(wg.shape, full3),
                  pl.BlockSpec((1, lanes), full2)],
        out_specs=[pl.BlockSpec((1, tq, e), tok)] * 4 + [pl.BlockSpec((1, tq, lanes), tok)],
        out_shape=[_sds((b, l, e), BF16)] * 4 + [_sds((b, l, lanes), F32)],
        compiler_params=_params("arbitrary", "arbitrary"),
        name="ml_qkv",
    )(xm, xm, xm, conv_w, conv_b.reshape(1, e), wq, wk, wv, wg, bg)


def _ml_gates_kernel(gp_ref, cb_ref, gg_ref, at_ref, *, nf):
    x = gp_ref[0]
    t, lanes = x.shape
    lf = jax.nn.log_sigmoid(x)
    ipre = pltpu.roll(x, nf, 1)
    ri = lax.broadcasted_iota(jnp.int32, (t, t), 0)
    ci = lax.broadcasted_iota(jnp.int32, (t, t), 1)
    lower = jnp.where(ci <= ri, 1.0, 0.0)
    upper = jnp.where(ci >= ri, 1.0, 0.0)
    cum_f = jnp.dot(lower, lf, preferred_element_type=F32, precision=HIGHEST)
    cum_b = jnp.dot(upper, lf, preferred_element_type=F32, precision=HIGHEST)
    fwd = lax.broadcasted_iota(jnp.int32, (t, lanes), 1) < 2 * nf
    cum = jnp.where(fwd, cum_f, cum_b)
    tot = jnp.where(fwd[:1], cum_f[t - 1:t, :], cum_b[0:1, :])
    cb_ref[0] = cum
    gg_ref[0] = tot - cum + ipre
    at_ref[0, 0] = (ipre - cum).T


def _ml_gates(gp, nheads):
    b, l, lanes = gp.shape
    t = CHUNK
    nc = l // t
    return pl.pallas_call(
        functools.partial(_ml_gates_kernel, nf=nheads),
        grid=(b, nc),
        in_specs=[pl.BlockSpec((1, t, lanes), lambda i, j: (i, j, 0))],
        out_specs=[pl.BlockSpec((1, t, lanes), lambda i, j: (i, j, 0)),
                   pl.BlockSpec((1, t, lanes), lambda i, j: (i, j, 0)),
                   pl.BlockSpec((1, 1, lanes, t), lambda i, j: (i, j, 0, 0))],
        out_shape=[_sds((b, l, lanes), F32), _sds((b, l, lanes), F32), _sds((b, nc, lanes, t), F32)],
        compiler_params=_params("arbitrary", "arbitrary"),
        name="ml_gates",
    )(gp)


def _ml_main_kernel(q_ref, k_ref, v_ref, cb_ref, gg_ref, at_ref, xc_ref, z_ref, x_ref, gate_ref,
                    gnw_ref, skip_ref, wout_ref, o_ref, c_scr, n_scr, m_scr, hf_scr, *, nc, nh):
    dirn = pl.program_id(1)
    step = pl.program_id(2)
    t = q_ref.shape[1]
    dh = q_ref.shape[2] // nh
    scale = float(dh) ** -0.5

    @pl.when(step == 0)
    def _():
        c_scr[...] = jnp.zeros_like(c_scr)
        n_scr[...] = jnp.zeros_like(n_scr)
        m_scr[...] = jnp.zeros_like(m_scr)

    ri = lax.broadcasted_iota(jnp.int32, (t, t), 0)
    ci = lax.broadcasted_iota(jnp.int32, (t, t), 1)

    def head_out(h, d):
        cols = slice(h * dh, (h + 1) * dh)
        gcol = d * 2 * nh + nh + h
        q = q_ref[0, :, cols]
        k = k_ref[0, :, cols]
        v = v_ref[0, :, cols]
        cb = cb_ref[0, :, gcol:gcol + 1]
        gg = gg_ref[0, :, gcol:gcol + 1]
        a_row = at_ref[0, 0, gcol:gcol + 1, :]
        tot = cb[t - 1:t, :] if d == 0 else cb[0:1, :]
        m = m_scr[h][:, 0:1]
        c_st = c_scr[h]
        n_st = n_scr[h]

        keep = (ci <= ri) if d == 0 else (ci >= ri)
        dmat = jnp.where(keep, cb + a_row, -jnp.inf)
        inter = cb + m
        m_t = jnp.maximum(inter, jnp.max(dmat, axis=1, keepdims=True))
        w_intra = jnp.exp(dmat - m_t)
        w_inter = jnp.exp(inter - m_t)
        s = _nt_dot(q, k) * scale * w_intra
        num = jnp.dot(s.astype(BF16), v, preferred_element_type=F32)
        num = num + w_inter * jnp.dot(q, c_st.astype(BF16), preferred_element_type=F32)
        qn = jnp.sum(q.astype(F32) * n_st, axis=1, keepdims=True)
        nq = jnp.sum(s, axis=1, keepdims=True) + w_inter * qn
        hout = num / jnp.maximum(jnp.abs(nq), jnp.exp(-m_t))

        m_new = jnp.maximum(tot + m, jnp.max(gg, axis=0, keepdims=True))
        wk = jnp.exp(gg - m_new)
        decay = jnp.exp(tot + m - m_new)
        kw = k.astype(F32) * (scale * wk)
        c_scr[h] = decay * c_st + lax.dot_general(kw.astype(BF16), v, (((0,), (0,)), ((), ())),
                                                  preferred_element_type=F32)
        n_scr[h] = decay * n_st + jnp.sum(kw, axis=0, keepdims=True)
        m_scr[h] = jnp.broadcast_to(m_new, m_scr.shape[1:])
        return hout

    @pl.when(dirn == 0)
    def _():
        rows = pl.ds(pl.multiple_of(step * t, t), t)
        for h in range(nh):
            hf_scr[rows, h * dh:(h + 1) * dh] = head_out(h, 0)

    @pl.when(dirn == 1)
    def _():
        rows = pl.ds(pl.multiple_of((nc - 1 - step) * t, t), t)
        acc = None
        for h in range(nh):
            cols = slice(h * dh, (h + 1) * dh)
            hs = hf_scr[rows, cols] + head_out(h, 1)
            mu = jnp.mean(hs, axis=1, keepdims=True)
            cen = hs - mu
            var = jnp.mean(cen * cen, axis=1, keepdims=True)
            hn = cen * lax.rsqrt(var + LN_EPS)
            y = hn * gnw_ref[:, cols] + skip_ref[:, cols] * xc_ref[0, :, cols].astype(F32)
            y = y * _silu(z_ref[0, :, cols].astype(F32))
            part = jnp.dot(y.astype(BF16), wout_ref[cols, :], preferred_element_type=F32)
            acc = part if acc is None else acc + part
        o_ref[0] = x_ref[0] + gate_ref[0] * acc


def _ml_main(q, k, v, cb, gg, at, xc, z, x, gate, gn_w, skip, w_out, nheads):
    b, l, e = q.shape
    d = x.shape[-1]
    t = CHUNK
    nc = l // t
    lanes = cb.shape[-1]
    dh = e // nheads
    chunk = lambda i, dd, s: (i, s + dd * (nc - 1 - 2 * s), 0)
    chunk4 = lambda i, dd, s: (i, s + dd * (nc - 1 - 2 * s), 0, 0)
    late = lambda i, dd, s: (i, (nc - 1) - dd * s, 0)
    vec = lambda i, dd, s: (0, 0)
    return pl.pallas_call(
        functools.partial(_ml_main_kernel, nc=nc, nh=nheads),
        grid=(b, 2, nc),
        in_specs=[pl.BlockSpec((1, t, e), chunk),
                  pl.BlockSpec((1, t, e), chunk),
                  pl.BlockSpec((1, t, e), chunk),
                  pl.BlockSpec((1, t, lanes), chunk),
                  pl.BlockSpec((1, t, lanes), chunk),
                  pl.BlockSpec((1, 1, lanes, t), chunk4),
                  pl.BlockSpec((1, t, e), late),
                  pl.BlockSpec((1, t, e), late),
                  pl.BlockSpec((1, t, d), late),
                  pl.BlockSpec((1, 1, d), lambda i, dd, s: (i, 0, 0)),
                  pl.BlockSpec((1, e), vec),
                  pl.BlockSpec((1, e), vec),
                  pl.BlockSpec((e, d), vec)],
        out_specs=pl.BlockSpec((1, t, d), late),
        out_shape=_sds((b, l, d), F32),
        scratch_shapes=[pltpu.VMEM((nheads, dh, dh), F32),
                        pltpu.VMEM((nheads, 1, dh), F32),
                        pltpu.VMEM((nheads, 1, V7X_LANES), F32),
                        pltpu.VMEM((l, e), F32)],
        compiler_params=_params("arbitrary", "arbitrary", "arbitrary"),
        name="ml_main",
    )(q, k, v, cb, gg, at, xc, z, x, gate, gn_w.reshape(1, e), skip.reshape(1, e), w_out.astype(BF16))


def _ml_layer(x, shift, scale, gate, g, w_in, conv_w, conv_b, w_q, w_k, w_v, w_gates, b_gates,
              gn_w, skip, w_out):
    nheads = b_gates.shape[0] // 4
    xm, z = _ml_pre(x, shift, scale, g, w_in.astype(BF16))
    q, k, v, xc, gp = _ml_qkv(xm, conv_w, conv_b, w_q, w_k, w_v, w_gates, b_gates)
    cb, gg, at = _ml_gates(gp, nheads)
    return _ml_main(q, k, v, cb, gg, at, xc, z, x, gate, gn_w, skip, w_out, nheads)


def _final_norm_kernel(x_ref, g_ref, o_ref):
    x = x_ref[0]
    ms = jnp.mean(x * x, axis=-1, keepdims=True)
    o_ref[0] = x * lax.rsqrt(ms + RMS_EPS) * g_ref[...]


def _final_norm(x, g):
    b, l, d = x.shape
    tl = min(TOKEN_TILE, l)
    return pl.pallas_call(
        _final_norm_kernel,
        grid=(b, l // tl),
        in_specs=[pl.BlockSpec((1, tl, d), lambda i, j: (i, j, 0)),
                  pl.BlockSpec((1, d), lambda i, j: (0, 0))],
        out_specs=pl.BlockSpec((1, tl, d), lambda i, j: (i, j, 0)),
        out_shape=_sds((b, l, d), F32),
        compiler_params=_params("arbitrary", "arbitrary"),
        name="final_norm",
    )(x, g.reshape(1, d))


def kernel(x, c, ada_w, ada_b, norm_g, s5_w_in, s5_lam_re, s5_lam_im, s5_log_dt, s5_b_re, s5_b_im, s5_c_re, s5_c_im, s5_d, s5_w_glu, s5_b_glu, s5_w_out, ml_w_in, ml_conv_w, ml_conv_b, ml_w_q, ml_w_k, ml_w_v, ml_w_gates, ml_b_gates, ml_gn_w, ml_skip, ml_w_out, final_g):
    depth = ada_w.shape[0]
    d = x.shape[-1]
    assert x.shape[1] % CHUNK == 0 and CHUNK & (CHUNK - 1) == 0
    mod = _ada(c, ada_w, ada_b)
    for i in range(depth):
        shift = mod[i, :, None, 0:d]
        scale = mod[i, :, None, d:2 * d]
        gate = mod[i, :, None, 2 * d:3 * d]
        g = norm_g[i].reshape(1, d)
        j = i // 2
        if i % 2 == 0:
            x = _s5_layer(x, shift, scale, gate, g, s5_w_in[j], s5_lam_re[j], s5_lam_im[j], s5_log_dt[j],
                          s5_b_re[j], s5_b_im[j], s5_c_re[j], s5_c_im[j], s5_d[j], s5_w_glu[j],
                          s5_b_glu[j], s5_w_out[j])
        else:
            x = _ml_layer(x, shift, scale, gate, g, ml_w_in[j], ml_conv_w[j], ml_conv_b[j], ml_w_q[j],
                          ml_w_k[j], ml_w_v[j], ml_w_gates[j], ml_b_gates[j], ml_gn_w[j], ml_skip[j],
                          ml_w_out[j])
    return _final_norm(x, final_g)
```

```python
import functools

import jax
import jax.numpy as jnp
from jax import lax
from jax.experimental import pallas as pl
from jax.experimental.pallas import tpu as pltpu

F32 = jnp.float32
BF16 = jnp.bfloat16
U32 = jnp.uint32
HIGHEST = lax.Precision.HIGHEST

RMS_EPS = 1e-6
LN_EPS = 1e-5
V7X_LANES = 128
V7X_SUBLANES = 8
V7X_MXU_DIM = 256
VMEM_LIMIT_BYTES = 56 * 1024 * 1024
S5_CHUNK = V7X_LANES
S5_TILE = V7X_SUBLANES * S5_CHUNK
S5_SUB = 512
ML_CHUNK = V7X_MXU_DIM
TOKEN_TILE = 512


def _params(*sem):
    return pltpu.CompilerParams(dimension_semantics=sem, vmem_limit_bytes=VMEM_LIMIT_BYTES)


def _sds(shape, dtype):
    return jax.ShapeDtypeStruct(shape, dtype)


def _silu(v):
    return v * jax.nn.sigmoid(v)


def _nt_dot(a, b, precision=None):
    return lax.dot_general(a, b, (((1,), (1,)), ((), ())), preferred_element_type=F32,
                           precision=precision)


def _cat(parts, axis):
    return parts[0] if len(parts) == 1 else jnp.concatenate(parts, axis=axis)


def _rms(x, g):
    ms = jnp.mean(x * x, axis=-1, keepdims=True)
    return x * lax.rsqrt(ms + RMS_EPS) * g


def _modulated_norm(x, g, shift, scale):
    return _rms(x, g) * (1.0 + scale) + shift


def _ada_kernel(c_ref, w_ref, b_ref, o_ref):
    sc = _silu(c_ref[...])
    o_ref[0] = jnp.dot(sc, w_ref[0], preferred_element_type=F32, precision=HIGHEST) + b_ref[0]


def _ada(c, ada_w, ada_b):
    depth, d, d3 = ada_w.shape
    b = c.shape[0]
    return pl.pallas_call(
        _ada_kernel,
        grid=(depth, d3 // d),
        in_specs=[pl.BlockSpec((b, d), lambda i, j: (0, 0)),
                  pl.BlockSpec((1, d, d), lambda i, j: (i, 0, j)),
                  pl.BlockSpec((1, 1, d), lambda i, j: (i, 0, j))],
        out_specs=pl.BlockSpec((1, b, d), lambda i, j: (i, 0, j)),
        out_shape=_sds((depth, b, d3), F32),
        compiler_params=_params("arbitrary", "arbitrary"),
        name="ada_mod",
    )(c, ada_w, ada_b.reshape(depth, 1, d3))


def _s5_pre_kernel(x_ref, shift_ref, scale_ref, g_ref, wt_ref, ut_ref, zt_ref, *, e, t, sub):
    for s in range(x_ref.shape[1] // sub):
        h = _modulated_norm(x_ref[0, s * sub:(s + 1) * sub, :], g_ref[...], shift_ref[0], scale_ref[0])
        uz = _nt_dot(wt_ref[...], h.astype(BF16))
        for k in range(sub // t):
            ut_ref[:, s * (sub // t) + k, :] = uz[:e, k * t:(k + 1) * t]
        zt_ref[0, :, s * sub:(s + 1) * sub] = uz[e:].astype(BF16)


def _s5_pre(x, shift, scale, g, w_in_t):
    b, l, d = x.shape
    e = w_in_t.shape[0] // 2
    t, tl = S5_CHUNK, S5_TILE
    steps = l // tl
    k = tl // t
    return pl.pallas_call(
        functools.partial(_s5_pre_kernel, e=e, t=t, sub=min(S5_SUB, tl)),
        grid=(b, steps),
        in_specs=[pl.BlockSpec((1, tl, d), lambda i, j: (i, j, 0)),
                  pl.BlockSpec((1, 1, d), lambda i, j: (i, 0, 0)),
                  pl.BlockSpec((1, 1, d), lambda i, j: (i, 0, 0)),
                  pl.BlockSpec((1, d), lambda i, j: (0, 0)),
                  pl.BlockSpec((2 * e, d), lambda i, j: (0, 0))],
        out_specs=[pl.BlockSpec((e, k, t), lambda i, j: (0, i * steps + j, 0)),
                   pl.BlockSpec((1, e, tl), lambda i, j: (i, 0, j))],
        out_shape=[_sds((e, b * l // t, t), F32), _sds((b, e, l), BF16)],
        compiler_params=_params("arbitrary", "arbitrary"),
        name="s5_pre",
    )(x, shift, scale, g, w_in_t)


def _abar(lre, lim, ldt):
    dt = jnp.exp(ldt)
    mag = jnp.exp(lre * dt)
    are = mag * jnp.cos(lim * dt)
    aim = mag * jnp.sin(lim * dt)
    den = lre * lre + lim * lim
    zre = ((are - 1.0) * lre + aim * lim) / den
    zim = (aim * lre - (are - 1.0) * lim) / den
    return are, aim, zre, zim


def _cplx_pow(ar, ai, expo, nbits):
    shape = expo.shape
    rr = jnp.ones(shape, F32)
    ri = jnp.zeros(shape, F32)
    pr = jnp.broadcast_to(ar, shape)
    pi = jnp.broadcast_to(ai, shape)
    for k in range(nbits):
        bit = ((expo >> k) & 1) == 1
        mr = jnp.where(bit, pr, 1.0)
        mi = jnp.where(bit, pi, 0.0)
        rr, ri = rr * mr - ri * mi, rr * mi + ri * mr
        if k + 1 < nbits:
            pr, pi = pr * pr - pi * pi, 2.0 * pr * pi
    return rr, ri


def _s5_operators(lre, lim, ldt, btre, btim, cre, cim, toep_scr, xmat_scr, ymat_scr, adec_scr, kw_scr,
                  *, t, n, p):
    n2 = 2 * n
    nbits = (t - 1).bit_length()
    lane_row = lax.broadcasted_iota(jnp.int32, (1, n2), 1)
    lane_tn = lax.broadcasted_iota(jnp.int32, (t, n2), 1)
    sub_tn = lax.broadcasted_iota(jnp.int32, (t, n2), 0)
    lane_p = lax.broadcasted_iota(jnp.int32, (p, n2), 1)
    lane_pp = lax.broadcasted_iota(jnp.int32, (p * p, n2), 1)
    first_row = lane_row < n
    first_tn = lane_tn < n

    k_halves = []
    k0_bwd = None
    for d in range(2):
        are, aim, zre, zim = _abar(lre[d], lim[d], ldt[d])

        pr, pi = are, aim
        for _ in range(t.bit_length() - 1):
            pr, pi = pr * pr - pi * pi, 2.0 * pr * pi
        adec_scr[2 * d:2 * d + 1, :] = pr
        adec_scr[2 * d + 1:2 * d + 2, :] = jnp.where(first_row, -pi, pi)

        bbt_re = zre * btre[d] - zim * btim[d]
        bbt_im = zre * btim[d] + zim * btre[d]

        asc_re, asc_im = _cplx_pow(are, aim, sub_tn, nbits)
        dsc_re, dsc_im = _cplx_pow(are, aim, t - 1 - sub_tn, nbits)
        up_re, up_im = (asc_re, asc_im) if d == 0 else (dsc_re, dsc_im)
        up_re, up_im = up_re * are - up_im * aim, up_re * aim + up_im * are

        rows = []
        for pp in range(p):
            c_re = cre[d][pp:pp + 1, :]
            c_im = cim[d][pp:pp + 1, :]
            cb_re = c_re * bbt_re - c_im * bbt_im
            cb_im = c_re * bbt_im + c_im * bbt_re
            rows.append(jnp.where(lane_p < n, cb_re, -cb_im))
        cbw = jnp.concatenate(rows, axis=0)
        lag_re, lag_im = (asc_re, asc_im) if d == 0 else (up_re, up_im)
        k_halves.append(_nt_dot(cbw, jnp.where(first_tn, lag_re, lag_im), precision=HIGHEST))
        if d == 1:
            k0_bwd = jnp.sum(jnp.where(lane_pp < n, cbw, 0.0), axis=1, keepdims=True)

        xt_re, xt_im = (dsc_re, dsc_im) if d == 0 else (asc_re, asc_im)
        for q in range(p):
            b_re = bbt_re[q:q + 1, :]
            b_im = bbt_im[q:q + 1, :]
            v1 = jnp.where(first_row, b_re, b_im)
            v2 = jnp.where(first_row, b_im, -b_re)
            xmat_scr[q * t:(q + 1) * t, d * n2:(d + 1) * n2] = (xt_re * v1 - xt_im * v2).astype(BF16)

        ya = jnp.where(first_tn, up_re, -up_im)
        yb = jnp.where(first_tn, up_im, up_re)
        for pp in range(p):
            ymat_scr[pp * t:(pp + 1) * t, d * n2:(d + 1) * n2] = (
                cre[d][pp:pp + 1, :] * ya - cim[d][pp:pp + 1, :] * yb).astype(BF16)

    lane_k = lax.broadcasted_iota(jnp.int32, (p * p, t), 1)
    kfull = jnp.concatenate([k_halves[0] + jnp.where(lane_k == 0, k0_bwd, 0.0), k_halves[1]], axis=1)
    kprev = pltpu.roll(kfull, 1, 1)
    lo = pltpu.bitcast(kfull.astype(BF16).astype(F32), U32) >> 16
    hi = pltpu.bitcast(kprev.astype(BF16).astype(F32), U32) & jnp.uint32(0xFFFF0000)
    words = hi | lo
    kw_scr[0] = words[:, :t]
    kw_scr[1] = words[:, t:]

    def q_body(q, carry):
        for pp in range(p):
            r = pp * p + q
            row = jnp.concatenate([kw_scr[0, pl.ds(r, 1), :], kw_scr[1, pl.ds(r, 1), :]], axis=1)
            full = jnp.broadcast_to(row, (t // 2, 2 * t))
            rolled = pltpu.roll(full, 0, 1, stride=2, stride_axis=0)
            toep_scr[pl.ds(pl.multiple_of(q * t, t), t), pp * t:(pp + 1) * t] = pltpu.bitcast(
                rolled[:, :t], BF16)
        return carry

    lax.fori_loop(0, p, q_body, 0)


def _s5_ssm_kernel(u_ref, lre, lim, ldt, btre, btim, cre, cim, y_ref,
                   toep_scr, xmat_scr, ymat_scr, adec_scr, kw_scr, lhs_scr, xs_scr, sin_scr,
                   *, nc, p, t, n):
    n2 = 2 * n
    bsz = u_ref.shape[1] // nc
    _s5_operators(lre, lim, ldt, btre, btim, cre, cim, toep_scr, xmat_scr, ymat_scr, adec_scr, kw_scr,
                  t=t, n=n, p=p)

    for q in range(p):
        lhs_scr[:, q * t:(q + 1) * t] = u_ref[q].astype(BF16)
    lhs = lhs_scr[...]

    for d in range(2):
        xs_scr[d] = jnp.dot(lhs, xmat_scr[:, d * n2:(d + 1) * n2], preferred_element_type=F32)

    for d in range(2):
        a1 = adec_scr[2 * d:2 * d + 1, :]
        a2 = adec_scr[2 * d + 1:2 * d + 2, :]
        s = jnp.zeros((bsz, n2), F32)
        for c in (range(nc) if d == 0 else reversed(range(nc))):
            rows = pl.ds(c, bsz, stride=nc)
            sin_scr[d, rows, :] = s
            s = a1 * s + a2 * pltpu.roll(s, n, 1) + xs_scr[d, rows, :]

    sin_f = sin_scr[0].astype(BF16)
    sin_b = sin_scr[1].astype(BF16)
    w = V7X_MXU_DIM
    for pp in range(p * t // w):
        cols = slice(pp * w, (pp + 1) * w)
        acc = jnp.dot(lhs, toep_scr[:, cols], preferred_element_type=F32)
        acc = acc + _nt_dot(sin_f, ymat_scr[cols, 0:n2])
        acc = acc + _nt_dot(sin_b, ymat_scr[cols, n2:2 * n2])
        for h in range(w // t):
            y_ref[pp * (w // t) + h] = acc[:, h * t:(h + 1) * t]


def _s5_ssm(ut, nc, lam_re, lam_im, log_dt, b_re, b_im, c_re, c_im):
    e, bc, t = ut.shape
    _, g, n, p = b_re.shape
    n2 = 2 * n
    pt = p * t
    dup = lambda a: jnp.concatenate([a, a], axis=-1)
    lre = dup(lam_re)[:, :, None, :]
    lim = dup(lam_im)[:, :, None, :]
    ldt = log_dt[:, :, None, None]
    btre = dup(jnp.swapaxes(b_re, -1, -2))
    btim = dup(jnp.swapaxes(b_im, -1, -2))
    cre = dup(c_re)
    cim = dup(c_im)
    prm = (lre, lim, ldt, btre, btim, cre, cim)

    def per_group(a):
        return pl.BlockSpec((2, None) + a.shape[2:], lambda i: (0, i, 0, 0))

    return pl.pallas_call(
        functools.partial(_s5_ssm_kernel, nc=nc, p=p, t=t, n=n),
        grid=(g,),
        in_specs=[pl.BlockSpec((p, bc, t), lambda i: (i, 0, 0))] + [per_group(a) for a in prm],
        out_specs=pl.BlockSpec((p, bc, t), lambda i: (i, 0, 0)),
        out_shape=_sds((e, bc, t), F32),
        scratch_shapes=[pltpu.VMEM((pt, pt), BF16),
                        pltpu.VMEM((pt, 2 * n2), BF16),
                        pltpu.VMEM((pt, 2 * n2), BF16),
                        pltpu.VMEM((4, n2), F32),
                        pltpu.VMEM((2, p * p, t), U32),
                        pltpu.VMEM((bc, pt), BF16),
                        pltpu.VMEM((2, bc, n2), F32),
                        pltpu.VMEM((2, bc, n2), F32)],
        compiler_params=_params("arbitrary"),
        name="s5_ssm",
    )(ut, *prm)


def _s5_post_kernel(y_ref, u_ref, zt_ref, x_ref, gate_ref, dcol_ref, bcol_ref, wgt_ref, wot_ref,
                    o_ref, *, t, sub):
    k = sub // t
    reps = sub // dcol_ref.shape[1]
    dfull = _cat([dcol_ref[...]] * reps, 1)
    bfull = _cat([bcol_ref[...]] * reps, 1)
    for s in range(x_ref.shape[1] // sub):
        tok = slice(s * sub, (s + 1) * sub)
        yt = _cat([y_ref[:, s * k + kk, :] for kk in range(k)], 1)
        ut = _cat([u_ref[:, s * k + kk, :] for kk in range(k)], 1)
        y = jax.nn.gelu(yt + dfull * ut)
        g = jnp.dot(wgt_ref[...], y.astype(BF16), preferred_element_type=F32) + bfull
        y = y * jax.nn.sigmoid(g)
        yz = y * _silu(zt_ref[0, :, tok].astype(F32))
        ot = jnp.dot(wot_ref[...], yz.astype(BF16), preferred_element_type=F32)
        o_ref[0, tok, :] = x_ref[0, tok, :] + gate_ref[0] * ot.T


def _s5_post(yt, ut, zt, x, gate, d_skip, b_glu, w_glu_t, w_out_t):
    b, l, d = x.shape
    e, bc, t = ut.shape
    tl = S5_TILE
    k = tl // t
    lanes = V7X_LANES
    dcol = jnp.broadcast_to(d_skip[:, None], (e, lanes))
    bcol = jnp.broadcast_to(b_glu[:, None], (e, lanes))
    steps = l // tl
    return pl.pallas_call(
        functools.partial(_s5_post_kernel, t=t, sub=min(S5_SUB, tl)),
        grid=(b, steps),
        in_specs=[pl.BlockSpec((e, k, t), lambda i, j: (0, i * steps + j, 0)),
                  pl.BlockSpec((e, k, t), lambda i, j: (0, i * steps + j, 0)),
                  pl.BlockSpec((1, e, tl), lambda i, j: (i, 0, j)),
                  pl.BlockSpec((1, tl, d), lambda i, j: (i, j, 0)),
                  pl.BlockSpec((1, 1, d), lambda i, j: (i, 0, 0)),
                  pl.BlockSpec((e, lanes), lambda i, j: (0, 0)),
                  pl.BlockSpec((e, lanes), lambda i, j: (0, 0)),
                  pl.BlockSpec((e, e), lambda i, j: (0, 0)),
                  pl.BlockSpec((d, e), lambda i, j: (0, 0))],
        out_specs=pl.BlockSpec((1, tl, d), lambda i, j: (i, j, 0)),
        out_shape=_sds((b, l, d), F32),
        compiler_params=_params("arbitrary", "arbitrary"),
        name="s5_post",
    )(yt, ut, zt, x, gate, dcol, bcol, w_glu_t, w_out_t)


def _s5_layer(x, shift, scale, gate, g, w_in, lam_re, lam_im, log_dt, b_re, b_im, c_re, c_im,
              d_skip, w_glu, b_glu, w_out):
    l = x.shape[1]
    assert l % S5_TILE == 0
    ut, zt = _s5_pre(x, shift, scale, g, w_in.T.astype(BF16))
    yt = _s5_ssm(ut, l // S5_CHUNK, lam_re, lam_im, log_dt, b_re, b_im, c_re, c_im)
    return _s5_post(yt, ut, zt, x, gate, d_skip, b_glu, w_glu.T.astype(BF16), w_out.T.astype(BF16))


def _ml_pre_kernel(x_ref, shift_ref, scale_ref, g_ref, w_ref, xm_ref, z_ref, *, e):
    h = _modulated_norm(x_ref[0], g_ref[...], shift_ref[0], scale_ref[0])
    xz = jnp.dot(h.astype(BF16), w_ref[...], preferred_element_type=F32)
    xm_ref[0] = xz[:, :e]
    z_ref[0] = xz[:, e:].astype(BF16)


def _ml_pre(x, shift, scale, g, w_in):
    b, l, d = x.shape
    e = w_in.shape[1] // 2
    tl = min(TOKEN_TILE, l)
    return pl.pallas_call(
        functools.partial(_ml_pre_kernel, e=e),
        grid=(b, l // tl),
        in_specs=[pl.BlockSpec((1, tl, d), lambda i, j: (i, j, 0)),
                  pl.BlockSpec((1, 1, d), lambda i, j: (i, 0, 0)),
                  pl.BlockSpec((1, 1, d), lambda i, j: (i, 0, 0)),
                  pl.BlockSpec((1, d), lambda i, j: (0, 0)),
                  pl.BlockSpec((d, 2 * e), lambda i, j: (0, 0))],
        out_specs=[pl.BlockSpec((1, tl, e), lambda i, j: (i, j, 0)),
                   pl.BlockSpec((1, tl, e), lambda i, j: (i, j, 0))],
        out_shape=[_sds((b, l, e), F32), _sds((b, l, e), BF16)],
        compiler_params=_params("arbitrary", "arbitrary"),
        name="ml_pre",
    )(x, shift, scale, g, w_in)


HALO = V7X_SUBLANES


def _ml_qkv_kernel(xm_ref, prev_ref, next_ref, cw_ref, cb_ref, wq_ref, wk_ref, wv_ref, wg_ref, bg_ref,
                   q_ref, k_ref, v_ref, xc_ref, gp_ref, *, steps, taps):
    j = pl.program_id(1)
    xm = xm_ref[0]
    tq, e = xm.shape
    prev = jnp.where(j == 0, 0.0, prev_ref[0])
    nxt = jnp.where(j == steps - 1, 0.0, next_ref[0])
    xe = jnp.concatenate([prev, xm, nxt], axis=0)
    pad = taps // 2
    acc = jnp.broadcast_to(cb_ref[...], (tq, e))
    for kk in range(taps):
        off = HALO + kk - pad
        acc = acc + xe[off:off + tq, :] * cw_ref[kk:kk + 1, :]
    xc = _silu(acc)
    xc_b = xc.astype(BF16)
    xm_b = xm.astype(BF16)
    tile = wq_ref.shape[-1]
    qs, ks, vs = [], [], []
    for i in range(e // tile):
        cols = slice(i * tile, (i + 1) * tile)
        qs.append(jnp.dot(xc_b[:, cols], wq_ref[i], preferred_element_type=F32))
        ks.append(jnp.dot(xc_b[:, cols], wk_ref[i], preferred_element_type=F32))
        vs.append(jnp.dot(xm_b[:, cols], wv_ref[i], preferred_element_type=F32))
    q = jnp.concatenate(qs, axis=1).astype(BF16)
    k = jnp.concatenate(ks, axis=1).astype(BF16)
    v = jnp.concatenate(vs, axis=1).astype(BF16)
    gp = (jnp.dot(q, wg_ref[0], preferred_element_type=F32)
          + jnp.dot(k, wg_ref[1], preferred_element_type=F32)
          + jnp.dot(v, wg_ref[2], preferred_element_type=F32) + bg_ref[...])
    q_ref[0] = q
    k_ref[0] = k
    v_ref[0] = v
    xc_ref[0] = xc_b
    gp_ref[0] = gp


def _blockdiag_tiles(w, tile):
    nblk, bs, _ = w.shape
    per = tile // bs
    w4 = w.reshape(nblk // per, per, bs, bs)
    eye = jnp.eye(per, dtype=w.dtype)
    return jnp.einsum('tirc,ij->tirjc', w4, eye).reshape(nblk // per, tile, tile)


def _ml_qkv(xm, conv_w, conv_b, w_q, w_k, w_v, w_gates, b_gates):
    b, l, e = xm.shape
    tq = min(TOKEN_TILE, l)
    steps = l // tq
    taps = conv_w.shape[0]
    tile = V7X_MXU_DIM
    lanes = V7X_LANES
    ng = w_gates.shape[-1]
    wq = _blockdiag_tiles(w_q, tile).astype(BF16)
    wk = _blockdiag_tiles(w_k, tile).astype(BF16)
    wv = _blockdiag_tiles(w_v, tile).astype(BF16)
    wg = jnp.pad(w_gates, ((0, 0), (0, 0), (0, lanes - ng))).astype(BF16)
    bg = jnp.pad(b_gates, (0, lanes - ng)).reshape(1, lanes)
    hb = tq // HALO
    last = l // HALO - 1
    tok = lambda i, j: (i, j, 0)
    full3 = lambda i, j: (0, 0, 0)
    full2 = lambda i, j: (0, 0)
    return pl.pallas_call(
        functools.partial(_ml_qkv_kernel, steps=steps, taps=taps),
        grid=(b, steps),
        in_specs=[pl.BlockSpec((1, tq, e), tok),
                  pl.BlockSpec((1, HALO, e), lambda i, j: (i, jnp.maximum(j * hb - 1, 0), 0)),
                  pl.BlockSpec((1, HALO, e), lambda i, j: (i, jnp.minimum((j + 1) * hb, last), 0)),
                  pl.BlockSpec((taps, e), full2),
                  pl.BlockSpec((1, e), full2),
                  pl.BlockSpec(wq.shape, full3),
                  pl.BlockSpec(wk.shape, full3),
                  pl.BlockSpec(wv.shape, full3),
                  pl.BlockSpec(wg.shape, full3),
                  pl.BlockSpec((1, lanes), full2)],
        out_specs=[pl.BlockSpec((1, tq, e), tok)] * 4 + [pl.BlockSpec((1, tq, lanes), tok)],
        out_shape=[_sds((b, l, e), BF16)] * 4 + [_sds((b, l, lanes), F32)],
        compiler_params=_params("arbitrary", "arbitrary"),
        name="ml_qkv",
    )(xm, xm, xm, conv_w, conv_b.reshape(1, e), wq, wk, wv, wg, bg)


def _ml_gates_kernel(gp_ref, cb_ref, gg_ref, at_ref, *, nf, t):
    lanes = gp_ref.shape[2]
    ri = lax.broadcasted_iota(jnp.int32, (t, t), 0)
    ci = lax.broadcasted_iota(jnp.int32, (t, t), 1)
    lower = jnp.where(ci <= ri, 1.0, 0.0)
    upper = jnp.where(ci >= ri, 1.0, 0.0)
    fwd = lax.broadcasted_iota(jnp.int32, (t, lanes), 1) < 2 * nf
    fwd_row = lax.broadcasted_iota(jnp.int32, (1, lanes), 1) < 2 * nf
    for c in range(gp_ref.shape[1] // t):
        rows = slice(c * t, (c + 1) * t)
        x = gp_ref[0, rows, :]
        lf = jax.nn.log_sigmoid(x)
        ipre = pltpu.roll(x, nf, 1)
        cum_f = jnp.dot(lower, lf, preferred_element_type=F32, precision=HIGHEST)
        cum_b = jnp.dot(upper, lf, preferred_element_type=F32, precision=HIGHEST)
        cum = jnp.where(fwd, cum_f, cum_b)
        tot = jnp.where(fwd_row, cum_f[t - 1:t, :], cum_b[0:1, :])
        cb_ref[0, rows, :] = cum
        gg_ref[0, rows, :] = tot - cum + ipre
        at_ref[0, c] = (ipre - cum).T


def _ml_gates(gp, nheads):
    b, l, lanes = gp.shape
    t = ML_CHUNK
    nc = l // t
    return pl.pallas_call(
        functools.partial(_ml_gates_kernel, nf=nheads, t=t),
        grid=(b,),
        in_specs=[pl.BlockSpec((1, l, lanes), lambda i: (i, 0, 0))],
        out_specs=[pl.BlockSpec((1, l, lanes), lambda i: (i, 0, 0)),
                   pl.BlockSpec((1, l, lanes), lambda i: (i, 0, 0)),
                   pl.BlockSpec((1, nc, lanes, t), lambda i: (i, 0, 0, 0))],
        out_shape=[_sds((b, l, lanes), F32), _sds((b, l, lanes), F32), _sds((b, nc, lanes, t), F32)],
        compiler_params=_params("arbitrary"),
        name="ml_gates",
    )(gp)


def _ml_main_kernel(q_ref, k_ref, v_ref, cb_ref, gg_ref, at_ref, xc_ref, z_ref, x_ref, gate_ref,
                    gnw_ref, skip_ref, wout_ref, *rest, nc, nh, final):
    if final:
        fg_ref, o_ref, c_scr, n_scr, m_scr, hf_scr = rest
    else:
        o_ref, c_scr, n_scr, m_scr, hf_scr = rest
    dirn = pl.program_id(1)
    step = pl.program_id(2)
    t = q_ref.shape[1]
    dh = q_ref.shape[2] // nh
    scale = float(dh) ** -0.5

    @pl.when(step == 0)
    def _():
        c_scr[...] = jnp.zeros_like(c_scr)
        n_scr[...] = jnp.zeros_like(n_scr)
        m_scr[...] = jnp.zeros_like(m_scr)

    ri = lax.broadcasted_iota(jnp.int32, (t, t), 0)
    ci = lax.broadcasted_iota(jnp.int32, (t, t), 1)

    def head_out(h, d):
        cols = slice(h * dh, (h + 1) * dh)
        gcol = d * 2 * nh + nh + h
        q = q_ref[0, :, cols]
        k = k_ref[0, :, cols]
        v = v_ref[0, :, cols]
        cb = cb_ref[0, :, gcol:gcol + 1]
        gg = gg_ref[0, :, gcol:gcol + 1]
        a_row = at_ref[0, 0, gcol:gcol + 1, :]
        tot = cb[t - 1:t, :] if d == 0 else cb[0:1, :]
        m = m_scr[h][:, 0:1]
        c_st = c_scr[h]
        n_st = n_scr[h]

        keep = (ci <= ri) if d == 0 else (ci >= ri)
        dmat = jnp.where(keep, cb + a_row, -jnp.inf)
        inter = cb + m
        m_t = jnp.maximum(inter, jnp.max(dmat, axis=1, keepdims=True))
        w_intra = jnp.exp(dmat - m_t)
        w_inter = jnp.exp(inter - m_t)
        s = _nt_dot(q, k) * scale * w_intra
        num = jnp.dot(s.astype(BF16), v, preferred_element_type=F32)
        num = num + w_inter * jnp.dot(q, c_st.astype(BF16), preferred_element_type=F32)
        qn = jnp.sum(q.astype(F32) * n_st, axis=1, keepdims=True)
        nq = jnp.sum(s, axis=1, keepdims=True) + w_inter * qn
        hout = num / jnp.maximum(jnp.abs(nq), jnp.exp(-m_t))

        m_new = jnp.maximum(tot + m, jnp.max(gg, axis=0, keepdims=True))
        wk = jnp.exp(gg - m_new)
        decay = jnp.exp(tot + m - m_new)
        kw = k.astype(F32) * (scale * wk)
        c_scr[h] = decay * c_st + lax.dot_general(kw.astype(BF16), v, (((0,), (0,)), ((), ())),
                                                  preferred_element_type=F32)
        n_scr[h] = decay * n_st + jnp.sum(kw, axis=0, keepdims=True)
        m_scr[h] = jnp.broadcast_to(m_new, m_scr.shape[1:])
        return hout

    @pl.when(dirn == 0)
    def _():
        rows = pl.ds(pl.multiple_of(step * t, t), t)
        for h in range(nh):
            hf_scr[rows, h * dh:(h + 1) * dh] = head_out(h, 0)

    @pl.when(dirn == 1)
    def _():
        rows = pl.ds(pl.multiple_of((nc - 1 - step) * t, t), t)
        acc = None
        for h in range(nh):
            cols = slice(h * dh, (h + 1) * dh)
            hs = hf_scr[rows, cols] + head_out(h, 1)
            mu = jnp.mean(hs, axis=1, keepdims=True)
            cen = hs - mu
            var = jnp.mean(cen * cen, axis=1, keepdims=True)
            hn = cen * lax.rsqrt(var + LN_EPS)
            y = hn * gnw_ref[:, cols] + skip_ref[:, cols] * xc_ref[0, :, cols].astype(F32)
            y = y * _silu(z_ref[0, :, cols].astype(F32))
            part = jnp.dot(y.astype(BF16), wout_ref[cols, :], preferred_element_type=F32)
            acc = part if acc is None else acc + part
        out = x_ref[0] + gate_ref[0] * acc
        o_ref[0] = _rms(out, fg_ref[...]) if final else out


def _ml_main(q, k, v, cb, gg, at, xc, z, x, gate, gn_w, skip, w_out, nheads, final_g):
    b, l, e = q.shape
    d = x.shape[-1]
    t = ML_CHUNK
    nc = l // t
    lanes = cb.shape[-1]
    dh = e // nheads
    chunk = lambda i, dd, s: (i, s + dd * (nc - 1 - 2 * s), 0)
    chunk4 = lambda i, dd, s: (i, s + dd * (nc - 1 - 2 * s), 0, 0)
    late = lambda i, dd, s: (i, (nc - 1) - dd * s, 0)
    vec = lambda i, dd, s: (0, 0)
    final = final_g is not None
    extra_specs = [pl.BlockSpec((1, d), vec)] if final else []
    extra_args = [final_g.reshape(1, d)] if final else []
    return pl.pallas_call(
        functools.partial(_ml_main_kernel, nc=nc, nh=nheads, final=final),
        grid=(b, 2, nc),
        in_specs=[pl.BlockSpec((1, t, e), chunk),
                  pl.BlockSpec((1, t, e), chunk),
                  pl.BlockSpec((1, t, e), chunk),
                  pl.BlockSpec((1, t, lanes), chunk),
                  pl.BlockSpec((1, t, lanes), chunk),
                  pl.BlockSpec((1, 1, lanes, t), chunk4),
                  pl.BlockSpec((1, t, e), late),
                  pl.BlockSpec((1, t, e), late),
                  pl.BlockSpec((1, t, d), late),
                  pl.BlockSpec((1, 1, d), lambda i, dd, s: (i, 0, 0)),
                  pl.BlockSpec((1, e), vec),
                  pl.BlockSpec((1, e), vec),
                  pl.BlockSpec((e, d), vec)] + extra_specs,
        out_specs=pl.BlockSpec((1, t, d), late),
        out_shape=_sds((b, l, d), F32),
        scratch_shapes=[pltpu.VMEM((nheads, dh, dh), F32),
                        pltpu.VMEM((nheads, 1, dh), F32),
                        pltpu.VMEM((nheads, 1, V7X_LANES), F32),
                        pltpu.VMEM((l, e), F32)],
        compiler_params=_params("arbitrary", "arbitrary", "arbitrary"),
        name="ml_main",
    )(q, k, v, cb, gg, at, xc, z, x, gate, gn_w.reshape(1, e), skip.reshape(1, e), w_out.astype(BF16),
      *extra_args)


def _ml_layer(x, shift, scale, gate, g, w_in, conv_w, conv_b, w_q, w_k, w_v, w_gates, b_gates,
              gn_w, skip, w_out, final_g):
    assert x.shape[1] % ML_CHUNK == 0
    nheads = b_gates.shape[0] // 4
    xm, z = _ml_pre(x, shift, scale, g, w_in.astype(BF16))
    q, k, v, xc, gp = _ml_qkv(xm, conv_w, conv_b, w_q, w_k, w_v, w_gates, b_gates)
    cb, gg, at = _ml_gates(gp, nheads)
    return _ml_main(q, k, v, cb, gg, at, xc, z, x, gate, gn_w, skip, w_out, nheads, final_g)


def _final_norm_kernel(x_ref, g_ref, o_ref):
    o_ref[0] = _rms(x_ref[0], g_ref[...])


def _final_norm(x, g):
    b, l, d = x.shape
    tl = min(TOKEN_TILE, l)
    return pl.pallas_call(
        _final_norm_kernel,
        grid=(b, l // tl),
        in_specs=[pl.BlockSpec((1, tl, d), lambda i, j: (i, j, 0)),
                  pl.BlockSpec((1, d), lambda i, j: (0, 0))],
        out_specs=pl.BlockSpec((1, tl, d), lambda i, j: (i, j, 0)),
        out_shape=_sds((b, l, d), F32),
        compiler_params=_params("arbitrary", "arbitrary"),
        name="final_norm",
    )(x, g.reshape(1, d))


def kernel(x, c, ada_w, ada_b, norm_g, s5_w_in, s5_lam_re, s5_lam_im, s5_log_dt, s5_b_re, s5_b_im, s5_c_re, s5_c_im, s5_d, s5_w_glu, s5_b_glu, s5_w_out, ml_w_in, ml_conv_w, ml_conv_b, ml_w_q, ml_w_k, ml_w_v, ml_w_gates, ml_b_gates, ml_gn_w, ml_skip, ml_w_out, final_g):
    depth = ada_w.shape[0]
    d = x.shape[-1]
    mod = _ada(c, ada_w, ada_b)
    for i in range(depth):
        shift = mod[i, :, None, 0:d]
        scale = mod[i, :, None, d:2 * d]
        gate = mod[i, :, None, 2 * d:3 * d]
        g = norm_g[i].reshape(1, d)
        j = i // 2
        if i % 2 == 0:
            x = _s5_layer(x, shift, scale, gate, g, s5_w_in[j], s5_lam_re[j], s5_lam_im[j], s5_log_dt[j],
                          s5_b_re[j], s5_b_im[j], s5_c_re[j], s5_c_im[j], s5_d[j], s5_w_glu[j],
                          s5_b_glu[j], s5_w_out[j])
        else:
            x = _ml_layer(x, shift, scale, gate, g, ml_w_in[j], ml_conv_w[j], ml_conv_b[j], ml_w_q[j],
                          ml_w_k[j], ml_w_v[j], ml_w_gates[j], ml_b_gates[j], ml_gn_w[j], ml_skip[j],
                          ml_w_out[j], final_g if i == depth - 1 else None)
    return x if depth % 2 == 0 else _final_norm(x, final_g)
```

```python
import functools

import jax
import jax.numpy as jnp
from jax import lax
from jax.experimental import pallas as pl
from jax.experimental.pallas import tpu as pltpu

F32 = jnp.float32
BF16 = jnp.bfloat16
U32 = jnp.uint32
HIGHEST = lax.Precision.HIGHEST

RMS_EPS = 1e-6
LN_EPS = 1e-5
V7X_LANES = 128
V7X_SUBLANES = 8
V7X_MXU_DIM = 256
VMEM_LIMIT_BYTES = 56 * 1024 * 1024
S5_CHUNK = V7X_LANES
S5_TILE = V7X_SUBLANES * S5_CHUNK
S5_SUB = 512
ML_CHUNK = V7X_MXU_DIM
TOKEN_TILE = 512


def _params(*sem):
    return pltpu.CompilerParams(dimension_semantics=sem, vmem_limit_bytes=VMEM_LIMIT_BYTES)


def _sds(shape, dtype):
    return jax.ShapeDtypeStruct(shape, dtype)


def _sigmoid(v):
    return 0.5 * jnp.tanh(0.5 * v) + 0.5


def _silu(v):
    return v * _sigmoid(v)


def _nt_dot(a, b, precision=None):
    return lax.dot_general(a, b, (((1,), (1,)), ((), ())), preferred_element_type=F32,
                           precision=precision)


def _cat(parts, axis):
    return parts[0] if len(parts) == 1 else jnp.concatenate(parts, axis=axis)


def _rms(x, g):
    ms = jnp.mean(x * x, axis=-1, keepdims=True)
    return x * lax.rsqrt(ms + RMS_EPS) * g


def _modulated_norm(x, g, shift, scale):
    return _rms(x, g) * (1.0 + scale) + shift


def _ada_kernel(c_ref, w_ref, b_ref, o_ref):
    sc = _silu(c_ref[...])
    o_ref[0] = jnp.dot(sc, w_ref[0], preferred_element_type=F32, precision=HIGHEST) + b_ref[0]


def _ada(c, ada_w, ada_b):
    depth, d, d3 = ada_w.shape
    b = c.shape[0]
    return pl.pallas_call(
        _ada_kernel,
        grid=(depth, d3 // d),
        in_specs=[pl.BlockSpec((b, d), lambda i, j: (0, 0)),
                  pl.BlockSpec((1, d, d), lambda i, j: (i, 0, j)),
                  pl.BlockSpec((1, 1, d), lambda i, j: (i, 0, j))],
        out_specs=pl.BlockSpec((1, b, d), lambda i, j: (i, 0, j)),
        out_shape=_sds((depth, b, d3), F32),
        compiler_params=_params("arbitrary", "arbitrary"),
        name="ada_mod",
    )(c, ada_w, ada_b.reshape(depth, 1, d3))


def _relayout_pitch(e):
    pitch = e + V7X_SUBLANES
    assert pitch % (2 * V7X_SUBLANES) == V7X_SUBLANES
    return pitch


def _s5_pre_kernel(x_ref, shift_ref, scale_ref, g_ref, wt_ref, ut_ref, uzt_ref, rl_scr, *, e, t, sub, pitch):
    kk = sub // t
    for s in range(x_ref.shape[1] // sub):
        h = _modulated_norm(x_ref[0, s * sub:(s + 1) * sub, :], g_ref[...], shift_ref[0], scale_ref[0])
        uz = _nt_dot(wt_ref[...], h.astype(BF16))
        uzt_ref[0, :, s * sub:(s + 1) * sub] = uz.astype(BF16)
        for k in range(kk):
            c = s * kk + k
            rl_scr[c * pitch:c * pitch + e, :] = uz[:e, k * t:(k + 1) * t]

    def gather(ch, carry):
        ut_ref[ch] = rl_scr[pl.ds(ch, V7X_SUBLANES, stride=pitch), :]
        return carry

    lax.fori_loop(0, e, gather, 0, unroll=8)


def _s5_pre(x, shift, scale, g, w_in_t):
    b, l, d = x.shape
    e = w_in_t.shape[0] // 2
    t, tl = S5_CHUNK, S5_TILE
    steps = l // tl
    k = tl // t
    pitch = _relayout_pitch(e)
    return pl.pallas_call(
        functools.partial(_s5_pre_kernel, e=e, t=t, sub=min(S5_SUB, tl), pitch=pitch),
        grid=(b, steps),
        in_specs=[pl.BlockSpec((1, tl, d), lambda i, j: (i, j, 0)),
                  pl.BlockSpec((1, 1, d), lambda i, j: (i, 0, 0)),
                  pl.BlockSpec((1, 1, d), lambda i, j: (i, 0, 0)),
                  pl.BlockSpec((1, d), lambda i, j: (0, 0)),
                  pl.BlockSpec((2 * e, d), lambda i, j: (0, 0))],
        out_specs=[pl.BlockSpec((e, k, t), lambda i, j: (0, i * steps + j, 0)),
                   pl.BlockSpec((1, 2 * e, tl), lambda i, j: (i, 0, j))],
        out_shape=[_sds((e, b * l // t, t), F32), _sds((b, 2 * e, l), BF16)],
        scratch_shapes=[pltpu.VMEM((k * pitch, t), F32)],
        compiler_params=_params("arbitrary", "arbitrary"),
        name="s5_pre",
    )(x, shift, scale, g, w_in_t)


def _abar(lre, lim, ldt):
    dt = jnp.exp(ldt)
    mag = jnp.exp(lre * dt)
    are = mag * jnp.cos(lim * dt)
    aim = mag * jnp.sin(lim * dt)
    den = lre * lre + lim * lim
    zre = ((are - 1.0) * lre + aim * lim) / den
    zim = (aim * lre - (are - 1.0) * lim) / den
    return are, aim, zre, zim


def _cplx_pow(ar, ai, expo, nbits):
    shape = expo.shape
    rr = jnp.ones(shape, F32)
    ri = jnp.zeros(shape, F32)
    pr = jnp.broadcast_to(ar, shape)
    pi = jnp.broadcast_to(ai, shape)
    for k in range(nbits):
        bit = ((expo >> k) & 1) == 1
        mr = jnp.where(bit, pr, 1.0)
        mi = jnp.where(bit, pi, 0.0)
        rr, ri = rr * mr - ri * mi, rr * mi + ri * mr
        if k + 1 < nbits:
            pr, pi = pr * pr - pi * pi, 2.0 * pr * pi
    return rr, ri


def _s5_tables(prm, xmat_scr, ymat_scr, adec_scr, kw_scr, *, t, n, p):
    lre, lim, ldt, btre, btim, cre, cim = prm
    n2 = 2 * n
    nbits = (t - 1).bit_length()
    lane_row = lax.broadcasted_iota(jnp.int32, (1, n2), 1)
    lane_tn = lax.broadcasted_iota(jnp.int32, (t, n2), 1)
    sub_tn = lax.broadcasted_iota(jnp.int32, (t, n2), 0)
    lane_p = lax.broadcasted_iota(jnp.int32, (p, n2), 1)
    lane_pp = lax.broadcasted_iota(jnp.int32, (p * p, n2), 1)
    first_row = lane_row < n
    first_tn = lane_tn < n

    k_halves = []
    k0_bwd = None
    for d in range(2):
        are, aim, zre, zim = _abar(lre[d], lim[d], ldt[d])

        pr, pi = are, aim
        for _ in range(t.bit_length() - 1):
            pr, pi = pr * pr - pi * pi, 2.0 * pr * pi
        adec_scr[2 * d:2 * d + 1, :] = pr
        adec_scr[2 * d + 1:2 * d + 2, :] = jnp.where(first_row, -pi, pi)

        bbt_re = zre * btre[d] - zim * btim[d]
        bbt_im = zre * btim[d] + zim * btre[d]

        asc_re, asc_im = _cplx_pow(are, aim, sub_tn, nbits)
        dsc_re, dsc_im = _cplx_pow(are, aim, t - 1 - sub_tn, nbits)
        up_re, up_im = (asc_re, asc_im) if d == 0 else (dsc_re, dsc_im)
        up_re, up_im = up_re * are - up_im * aim, up_re * aim + up_im * are

        rows = []
        for pp in range(p):
            c_re = cre[d][pp:pp + 1, :]
            c_im = cim[d][pp:pp + 1, :]
            cb_re = c_re * bbt_re - c_im * bbt_im
            cb_im = c_re * bbt_im + c_im * bbt_re
            rows.append(jnp.where(lane_p < n, cb_re, -cb_im))
        cbw = jnp.concatenate(rows, axis=0)
        lag_re, lag_im = (asc_re, asc_im) if d == 0 else (up_re, up_im)
        k_halves.append(_nt_dot(cbw, jnp.where(first_tn, lag_re, lag_im), precision=HIGHEST))
        if d == 1:
            k0_bwd = jnp.sum(jnp.where(lane_pp < n, cbw, 0.0), axis=1, keepdims=True)

        xt_re, xt_im = (dsc_re, dsc_im) if d == 0 else (asc_re, asc_im)
        for q in range(p):
            b_re = bbt_re[q:q + 1, :]
            b_im = bbt_im[q:q + 1, :]
            v1 = jnp.where(first_row, b_re, b_im)
            v2 = jnp.where(first_row, b_im, -b_re)
            xmat_scr[q * t:(q + 1) * t, d * n2:(d + 1) * n2] = (xt_re * v1 - xt_im * v2).astype(BF16)

        ya = jnp.where(first_tn, up_re, -up_im)
        yb = jnp.where(first_tn, up_im, up_re)
        for pp in range(p):
            ymat_scr[pp * t:(pp + 1) * t, d * n2:(d + 1) * n2] = (
                cre[d][pp:pp + 1, :] * ya - cim[d][pp:pp + 1, :] * yb).astype(BF16)

    lane_k = lax.broadcasted_iota(jnp.int32, (p * p, t), 1)
    kfull = jnp.concatenate([k_halves[0] + jnp.where(lane_k == 0, k0_bwd, 0.0), k_halves[1]], axis=1)
    kprev = pltpu.roll(kfull, 1, 1)
    lo = pltpu.bitcast(kfull.astype(BF16).astype(F32), U32) >> 16
    hi = pltpu.bitcast(kprev.astype(BF16).astype(F32), U32) & jnp.uint32(0xFFFF0000)
    words = hi | lo
    kw_scr[0] = words[:, :t]
    kw_scr[1] = words[:, t:]


def _s5_toep_tile(kw_scr, toep_scr, q, pp, *, t, p):
    r = pp * p + q
    row = jnp.concatenate([kw_scr[0, pl.ds(r, 1), :], kw_scr[1, pl.ds(r, 1), :]], axis=1)
    full = jnp.broadcast_to(row, (t // 2, 2 * t))
    rolled = pltpu.roll(full, 0, 1, stride=2, stride_axis=0)
    start = q * t if isinstance(q, int) else pl.multiple_of(q * t, t)
    toep_scr[pl.ds(start, t), pp * t:(pp + 1) * t] = pltpu.bitcast(rolled[:, :t], BF16)


def _s5_ssm_kernel(u_ref, *refs, nc, p, t, n):
    cur_prm, nxt_prm, y_ref = refs[0:7], refs[7:14], refs[14]
    toep_scr, xmat_scr, ymat_scr, adec_scr, kw_scr, lhs_scr, xs_scr, xw_scr, sin_scr = refs[15:]
    n2 = 2 * n
    bsz = u_ref.shape[1] // nc
    g = pl.program_id(0)
    slot = lax.rem(g, 2)
    nslot = 1 - slot
    kw = dict(t=t, n=n, p=p)

    @pl.when(g == 0)
    def _():
        _s5_tables(cur_prm, xmat_scr.at[0], ymat_scr.at[0], adec_scr.at[0], kw_scr.at[0], **kw)

        def q_body(q, carry):
            for pp in range(p):
                _s5_toep_tile(kw_scr.at[0], toep_scr.at[0], q, pp, t=t, p=p)
            return carry

        lax.fori_loop(0, p, q_body, 0)

    _s5_tables(nxt_prm, xmat_scr.at[nslot], ymat_scr.at[nslot], adec_scr.at[nslot], kw_scr.at[nslot], **kw)

    for q in range(p):
        lhs_scr[:, q * t:(q + 1) * t] = u_ref[q].astype(BF16)
    lhs = lhs_scr[...]

    xmat = xmat_scr.at[slot]
    ymat = ymat_scr.at[slot]
    adec = adec_scr.at[slot]
    toep = toep_scr.at[slot]
    for d in range(2):
        xs = jnp.dot(lhs, xmat[:, d * n2:(d + 1) * n2], preferred_element_type=F32)
        xs_scr[d] = xs
        xw_scr[d] = pltpu.roll(xs, n, 1)

    for d in range(2):
        a1 = adec[2 * d:2 * d + 1, :]
        a2 = adec[2 * d + 1:2 * d + 2, :]
        s = jnp.zeros((bsz, n2), F32)
        sw = jnp.zeros((bsz, n2), F32)
        for c in (range(nc) if d == 0 else reversed(range(nc))):
            rows = pl.ds(c, bsz, stride=nc)
            sin_scr[d, rows, :] = s
            s, sw = (a1 * s + a2 * sw + xs_scr[d, rows, :],
                     a1 * sw - a2 * s + xw_scr[d, rows, :])

    sin_f = sin_scr[0].astype(BF16)
    sin_b = sin_scr[1].astype(BF16)
    w = V7X_MXU_DIM
    npanels = p * t // w
    tiles = [(q, pp) for q in range(p) for pp in range(p)]
    per_panel = len(tiles) // npanels
    for pn in range(npanels):
        cols = slice(pn * w, (pn + 1) * w)
        acc = jnp.dot(lhs, toep[:, cols], preferred_element_type=F32)
        acc = acc + _nt_dot(sin_f, ymat[cols, 0:n2])
        acc = acc + _nt_dot(sin_b, ymat[cols, n2:2 * n2])
        for h in range(w // t):
            y_ref[pn * (w // t) + h] = acc[:, h * t:(h + 1) * t]
        for q, pp in tiles[pn * per_panel:(pn + 1) * per_panel]:
            _s5_toep_tile(kw_scr.at[nslot], toep_scr.at[nslot], q, pp, t=t, p=p)


def _s5_ssm(ut, nc, lam_re, lam_im, log_dt, b_re, b_im, c_re, c_im):
    e, bc, t = ut.shape
    _, g, n, p = b_re.shape
    n2 = 2 * n
    pt = p * t
    dup = lambda a: jnp.concatenate([a, a], axis=-1)
    lre = dup(lam_re)[:, :, None, :]
    lim = dup(lam_im)[:, :, None, :]
    ldt = log_dt[:, :, None, None]
    btre = dup(jnp.swapaxes(b_re, -1, -2))
    btim = dup(jnp.swapaxes(b_im, -1, -2))
    cre = dup(c_re)
    cim = dup(c_im)
    prm = (lre, lim, ldt, btre, btim, cre, cim)

    def this_group(a):
        return pl.BlockSpec((2, None) + a.shape[2:], lambda i: (0, i, 0, 0))

    def next_group(a):
        return pl.BlockSpec((2, None) + a.shape[2:], lambda i: (0, jnp.minimum(i + 1, g - 1), 0, 0))

    return pl.pallas_call(
        functools.partial(_s5_ssm_kernel, nc=nc, p=p, t=t, n=n),
        grid=(g,),
        in_specs=([pl.BlockSpec((p, bc, t), lambda i: (i, 0, 0))]
                  + [this_group(a) for a in prm] + [next_group(a) for a in prm]),
        out_specs=pl.BlockSpec((p, bc, t), lambda i: (i, 0, 0)),
        out_shape=_sds((e, bc, t), F32),
        scratch_shapes=[pltpu.VMEM((2, pt, pt), BF16),
                        pltpu.VMEM((2, pt, 2 * n2), BF16),
                        pltpu.VMEM((2, pt, 2 * n2), BF16),
                        pltpu.VMEM((2, 4, n2), F32),
                        pltpu.VMEM((2, 2, p * p, t), U32),
                        pltpu.VMEM((bc, pt), BF16),
                        pltpu.VMEM((2, bc, n2), F32),
                        pltpu.VMEM((2, bc, n2), F32),
                        pltpu.VMEM((2, bc, n2), F32)],
        compiler_params=_params("arbitrary"),
        name="s5_ssm",
    )(ut, *prm, *prm)


def _s5_post_kernel(y_ref, uzt_ref, x_ref, gate_ref, dcol_ref, bcol_ref, wgt_ref, wot_ref,
                    o_ref, rl_scr, *, e, t, sub, pitch):
    def scatter(ch, carry):
        rl_scr[pl.ds(ch, V7X_SUBLANES, stride=pitch), :] = y_ref[ch]
        return carry

    lax.fori_loop(0, e, scatter, 0, unroll=8)

    k = sub // t
    reps = sub // dcol_ref.shape[1]
    dfull = _cat([dcol_ref[...]] * reps, 1)
    bfull = _cat([bcol_ref[...]] * reps, 1)
    for s in range(x_ref.shape[1] // sub):
        tok = slice(s * sub, (s + 1) * sub)
        yt = _cat([rl_scr[(s * k + kk) * pitch:(s * k + kk) * pitch + e, :] for kk in range(k)], 1)
        ut = uzt_ref[0, :e, tok].astype(F32)
        y = jax.nn.gelu(yt + dfull * ut)
        g = jnp.dot(wgt_ref[...], y.astype(BF16), preferred_element_type=F32) + bfull
        y = y * _sigmoid(g)
        z = uzt_ref[0, e:, tok].astype(F32)
        yz = y * _silu(z)
        ot = jnp.dot(wot_ref[...], yz.astype(BF16), preferred_element_type=F32)
        o_ref[0, tok, :] = x_ref[0, tok, :] + gate_ref[0] * ot.T


def _s5_post(yt, uzt, x, gate, d_skip, b_glu, w_glu_t, w_out_t):
    b, l, d = x.shape
    e, bc, t = yt.shape
    tl = S5_TILE
    k = tl // t
    lanes = V7X_LANES
    dcol = jnp.broadcast_to(d_skip[:, None], (e, lanes))
    bcol = jnp.broadcast_to(b_glu[:, None], (e, lanes))
    steps = l // tl
    pitch = _relayout_pitch(e)
    return pl.pallas_call(
        functools.partial(_s5_post_kernel, e=e, t=t, sub=min(S5_SUB, tl), pitch=pitch),
        grid=(b, steps),
        in_specs=[pl.BlockSpec((e, k, t), lambda i, j: (0, i * steps + j, 0)),
                  pl.BlockSpec((1, 2 * e, tl), lambda i, j: (i, 0, j)),
                  pl.BlockSpec((1, tl, d), lambda i, j: (i, j, 0)),
                  pl.BlockSpec((1, 1, d), lambda i, j: (i, 0, 0)),
                  pl.BlockSpec((e, lanes), lambda i, j: (0, 0)),
                  pl.BlockSpec((e, lanes), lambda i, j: (0, 0)),
                  pl.BlockSpec((e, e), lambda i, j: (0, 0)),
                  pl.BlockSpec((d, e), lambda i, j: (0, 0))],
        out_specs=pl.BlockSpec((1, tl, d), lambda i, j: (i, j, 0)),
        out_shape=_sds((b, l, d), F32),
        scratch_shapes=[pltpu.VMEM((k * pitch, t), F32)],
        compiler_params=_params("arbitrary", "arbitrary"),
        name="s5_post",
    )(yt, uzt, x, gate, dcol, bcol, w_glu_t, w_out_t)


def _s5_layer(x, shift, scale, gate, g, w_in, lam_re, lam_im, log_dt, b_re, b_im, c_re, c_im,
              d_skip, w_glu, b_glu, w_out):
    l = x.shape[1]
    assert l % S5_TILE == 0
    ut, uzt = _s5_pre(x, shift, scale, g, w_in.T.astype(BF16))
    yt = _s5_ssm(ut, l // S5_CHUNK, lam_re, lam_im, log_dt, b_re, b_im, c_re, c_im)
    return _s5_post(yt, uzt, x, gate, d_skip, b_glu, w_glu.T.astype(BF16), w_out.T.astype(BF16))


def _ml_pre_kernel(x_ref, shift_ref, scale_ref, g_ref, w_ref, xm_ref, z_ref, *, e):
    h = _modulated_norm(x_ref[0], g_ref[...], shift_ref[0], scale_ref[0])
    xz = jnp.dot(h.astype(BF16), w_ref[...], preferred_element_type=F32)
    xm_ref[0] = xz[:, :e]
    z_ref[0] = xz[:, e:].astype(BF16)


def _ml_pre(x, shift, scale, g, w_in):
    b, l, d = x.shape
    e = w_in.shape[1] // 2
    tl = min(TOKEN_TILE, l)
    return pl.pallas_call(
        functools.partial(_ml_pre_kernel, e=e),
        grid=(b, l // tl),
        in_specs=[pl.BlockSpec((1, tl, d), lambda i, j: (i, j, 0)),
                  pl.BlockSpec((1, 1, d), lambda i, j: (i, 0, 0)),
                  pl.BlockSpec((1, 1, d), lambda i, j: (i, 0, 0)),
                  pl.BlockSpec((1, d), lambda i, j: (0, 0)),
                  pl.BlockSpec((d, 2 * e), lambda i, j: (0, 0))],
        out_specs=[pl.BlockSpec((1, tl, e), lambda i, j: (i, j, 0)),
                   pl.BlockSpec((1, tl, e), lambda i, j: (i, j, 0))],
        out_shape=[_sds((b, l, e), F32), _sds((b, l, e), BF16)],
        compiler_params=_params("arbitrary", "arbitrary"),
        name="ml_pre",
    )(x, shift, scale, g, w_in)


HALO = V7X_SUBLANES


def _ml_qkv_kernel(xm_ref, prev_ref, next_ref, cw_ref, cb_ref, wq_ref, wk_ref, wkt_ref, wv_ref, wg_ref,
                   bg_ref, q_ref, kt_ref, v_ref, xc_ref, gp_ref, *, steps, taps):
    j = pl.program_id(1)
    xm = xm_ref[0]
    tq, e = xm.shape
    prev = jnp.where(j == 0, 0.0, prev_ref[0])
    nxt = jnp.where(j == steps - 1, 0.0, next_ref[0])
    xe = jnp.concatenate([prev, xm, nxt], axis=0)
    pad = taps // 2
    acc = jnp.broadcast_to(cb_ref[...], (tq, e))
    for kk in range(taps):
        off = HALO + kk - pad
        acc = acc + xe[off:off + tq, :] * cw_ref[kk:kk + 1, :]
    xc = _silu(acc)
    xc_b = xc.astype(BF16)
    xm_b = xm.astype(BF16)
    tile = wq_ref.shape[-1]
    qs, kts, vs = [], [], []
    gp = jnp.broadcast_to(bg_ref[...], (tq, bg_ref.shape[1]))
    for i in range(e // tile):
        cols = slice(i * tile, (i + 1) * tile)
        qs.append(jnp.dot(xc_b[:, cols], wq_ref[i], preferred_element_type=F32))
        kts.append(_nt_dot(wkt_ref[i], xc_b[:, cols]))
        vs.append(jnp.dot(xm_b[:, cols], wv_ref[i], preferred_element_type=F32))
        wkg = jnp.dot(wk_ref[i], wg_ref[1, cols, :], preferred_element_type=F32)
        gp = gp + jnp.dot(xc_b[:, cols], wkg.astype(BF16), preferred_element_type=F32)
    q = jnp.concatenate(qs, axis=1).astype(BF16)
    v = jnp.concatenate(vs, axis=1).astype(BF16)
    gp = (gp + jnp.dot(q, wg_ref[0], preferred_element_type=F32)
          + jnp.dot(v, wg_ref[2], preferred_element_type=F32))
    q_ref[0] = q
    kt_ref[0] = jnp.concatenate(kts, axis=0).astype(BF16)
    v_ref[0] = v
    xc_ref[0] = xc_b
    gp_ref[0] = gp


def _blockdiag_tiles(w, tile):
    nblk, bs, _ = w.shape
    per = tile // bs
    w4 = w.reshape(nblk // per, per, bs, bs)
    eye = jnp.eye(per, dtype=w.dtype)
    return jnp.einsum('tirc,ij->tirjc', w4, eye).reshape(nblk // per, tile, tile)


def _ml_qkv(xm, conv_w, conv_b, w_q, w_k, w_v, w_gates, b_gates):
    b, l, e = xm.shape
    tq = min(TOKEN_TILE, l)
    steps = l // tq
    taps = conv_w.shape[0]
    tile = V7X_MXU_DIM
    lanes = V7X_LANES
    ng = w_gates.shape[-1]
    wq = _blockdiag_tiles(w_q, tile).astype(BF16)
    wk = _blockdiag_tiles(w_k, tile).astype(BF16)
    wkt = jnp.swapaxes(wk, 1, 2)
    wv = _blockdiag_tiles(w_v, tile).astype(BF16)
    wg = jnp.pad(w_gates, ((0, 0), (0, 0), (0, lanes - ng))).astype(BF16)
    bg = jnp.pad(b_gates, (0, lanes - ng)).reshape(1, lanes)
    hb = tq // HALO
    last = l // HALO - 1
    tok = lambda i, j: (i, j, 0)
    full3 = lambda i, j: (0, 0, 0)
    full2 = lambda i, j: (0, 0)
    return pl.pallas_call(
        functools.partial(_ml_qkv_kernel, steps=steps, taps=taps),
        grid=(b, steps),
        in_specs=[pl.BlockSpec((1, tq, e), tok),
                  pl.BlockSpec((1, HALO, e), lambda i, j: (i, jnp.maximum(j * hb - 1, 0), 0)),
                  pl.BlockSpec((1, HALO, e), lambda i, j: (i, jnp.minimum((j + 1) * hb, last), 0)),
                  pl.BlockSpec((taps, e), full2),
                  pl.BlockSpec((1, e), full2),
                  pl.BlockSpec(wq.shape, full3),
                  pl.BlockSpec(wk.shape, full3),
                  pl.BlockSpec(wk.shape, full3),
                  pl.BlockSpec(wv.shape, full3),
                  pl.BlockSpec(wg.shape, full3),
                  pl.BlockSpec((1, lanes), full2)],
        out_specs=[pl.BlockSpec((1, tq, e), tok), pl.BlockSpec((1, e, tq), lambda i, j: (i, 0, j)),
                   pl.BlockSpec((1, tq, e), tok), pl.BlockSpec((1, tq, e), tok),
                   pl.BlockSpec((1, tq, lanes), tok)],
        out_shape=[_sds((b, l, e), BF16), _sds((b, e, l), BF16), _sds((b, l, e), BF16),
                   _sds((b, l, e), BF16), _sds((b, l, lanes), F32)],
        compiler_params=_params("arbitrary", "arbitrary"),
        name="ml_qkv",
    )(xm, xm, xm, conv_w, conv_b.reshape(1, e), wq, wk, wkt, wv, wg, bg)


def _ml_gates_kernel(gp_ref, cb_ref, gg_ref, at_ref, gt_ref, *, nf, t):
    lanes = gp_ref.shape[2]
    ri = lax.broadcasted_iota(jnp.int32, (t, t), 0)
    ci = lax.broadcasted_iota(jnp.int32, (t, t), 1)
    lower = jnp.where(ci <= ri, 1.0, 0.0)
    upper = jnp.where(ci >= ri, 1.0, 0.0)
    fwd = lax.broadcasted_iota(jnp.int32, (t, lanes), 1) < 2 * nf
    fwd_row = lax.broadcasted_iota(jnp.int32, (1, lanes), 1) < 2 * nf
    for c in range(gp_ref.shape[1] // t):
        rows = slice(c * t, (c + 1) * t)
        x = gp_ref[0, rows, :]
        lf = jax.nn.log_sigmoid(x)
        ipre = pltpu.roll(x, nf, 1)
        cum_f = jnp.dot(lower, lf, preferred_element_type=F32, precision=HIGHEST)
        cum_b = jnp.dot(upper, lf, preferred_element_type=F32, precision=HIGHEST)
        cum = jnp.where(fwd, cum_f, cum_b)
        tot = jnp.where(fwd_row, cum_f[t - 1:t, :], cum_b[0:1, :])
        gg = tot - cum + ipre
        cb_ref[0, rows, :] = cum
        gg_ref[0, rows, :] = gg
        at_ref[0, c] = (ipre - cum).T
        gt_ref[0, c] = gg.T


def _ml_gates(gp, nheads):
    b, l, lanes = gp.shape
    t = ML_CHUNK
    nc = l // t
    return pl.pallas_call(
        functools.partial(_ml_gates_kernel, nf=nheads, t=t),
        grid=(b,),
        in_specs=[pl.BlockSpec((1, l, lanes), lambda i: (i, 0, 0))],
        out_specs=[pl.BlockSpec((1, l, lanes), lambda i: (i, 0, 0)),
                   pl.BlockSpec((1, l, lanes), lambda i: (i, 0, 0)),
                   pl.BlockSpec((1, nc, lanes, t), lambda i: (i, 0, 0, 0)),
                   pl.BlockSpec((1, nc, lanes, t), lambda i: (i, 0, 0, 0))],
        out_shape=[_sds((b, l, lanes), F32), _sds((b, l, lanes), F32), _sds((b, nc, lanes, t), F32),
                   _sds((b, nc, lanes, t), F32)],
        compiler_params=_params("arbitrary"),
        name="ml_gates",
    )(gp)


def _ml_main_kernel(q_ref, kt_ref, v_ref, cb_ref, gg_ref, at_ref, gt_ref, xc_ref, z_ref, x_ref, gate_ref,
                    gnw_ref, skip_ref, wout_ref, *rest, nc, nh, final):
    if final:
        fg_ref, o_ref, c_scr, n_scr, m_scr, hf_scr = rest
    else:
        o_ref, c_scr, n_scr, m_scr, hf_scr = rest
    dirn = pl.program_id(1)
    step = pl.program_id(2)
    t = q_ref.shape[1]
    dh = q_ref.shape[2] // nh
    scale = float(dh) ** -0.5

    @pl.when(step == 0)
    def _():
        c_scr[...] = jnp.zeros_like(c_scr)
        n_scr[...] = jnp.zeros_like(n_scr)
        m_scr[...] = jnp.zeros_like(m_scr)

    ri = lax.broadcasted_iota(jnp.int32, (t, t), 0)
    ci = lax.broadcasted_iota(jnp.int32, (t, t), 1)
    ones_rows = jnp.ones((V7X_SUBLANES, t), BF16)

    def heads_out(d):
        keep = (ci <= ri) if d == 0 else (ci >= ri)
        stage = []
        for h in range(nh):
            cols = slice(h * dh, (h + 1) * dh)
            gcol = d * 2 * nh + nh + h
            q = q_ref[0, :, cols]
            kt = kt_ref[0, cols, :]
            cb = cb_ref[0, :, gcol:gcol + 1]
            gg = gg_ref[0, :, gcol:gcol + 1]
            a_row = at_ref[0, 0, gcol:gcol + 1, :]
            g_row = gt_ref[0, 0, gcol:gcol + 1, :]
            tot = cb[t - 1:t, :] if d == 0 else cb[0:1, :]
            m = m_scr[h][:, 0:1]
            dmat = jnp.where(keep, cb + a_row, -jnp.inf)
            inter = cb + m
            m_t = jnp.maximum(inter, jnp.max(dmat, axis=1, keepdims=True))
            m_new = jnp.maximum(tot + m, jnp.max(gg, axis=0, keepdims=True))
            stage.append(dict(
                q=q, kt=kt, m_t=m_t, m_new=m_new,
                w_intra=jnp.exp(dmat - m_t),
                w_inter=jnp.exp(inter - m_t),
                qk=jnp.dot(q, kt, preferred_element_type=F32),
                decay=jnp.exp(tot + m - m_new),
                wk_row=jnp.exp(g_row - m_new) * scale))
        for h in range(nh):
            cols = slice(h * dh, (h + 1) * dh)
            a = stage[h]
            v = v_ref[0, :, cols]
            c_st = c_scr[h]
            s = a['qk'] * scale * a['w_intra']
            qw = a['q'] * a['w_inter'].astype(BF16)
            num = jnp.dot(s.astype(BF16), v, preferred_element_type=F32)
            num = num + jnp.dot(qw, c_st.astype(BF16), preferred_element_type=F32)
            qn = jnp.sum(a['q'].astype(F32) * n_scr[h], axis=1, keepdims=True)
            nq = jnp.sum(s, axis=1, keepdims=True) + a['w_inter'] * qn
            a['hout'] = num / jnp.maximum(jnp.abs(nq), jnp.exp(-a['m_t']))
            kwt = a['kt'] * a['wk_row'].astype(BF16)
            a['c_new'] = a['decay'] * c_st + jnp.dot(kwt, v, preferred_element_type=F32)
            a['n_add'] = _nt_dot(ones_rows, kwt)[0:1, :]
        for h in range(nh):
            a = stage[h]
            c_scr[h] = a['c_new']
            n_scr[h] = a['decay'] * n_scr[h] + a['n_add']
            m_scr[h] = jnp.broadcast_to(a['m_new'], m_scr.shape[1:])
        return [a['hout'] for a in stage]

    @pl.when(dirn == 0)
    def _():
        rows = pl.ds(pl.multiple_of(step * t, t), t)
        for h, hout in enumerate(heads_out(0)):
            hf_scr[rows, h * dh:(h + 1) * dh] = hout

    @pl.when(dirn == 1)
    def _():
        rows = pl.ds(pl.multiple_of((nc - 1 - step) * t, t), t)
        acc = None
        for h, hout in enumerate(heads_out(1)):
            cols = slice(h * dh, (h + 1) * dh)
            hs = hf_scr[rows, cols] + hout
            mu = jnp.mean(hs, axis=1, keepdims=True)
            cen = hs - mu
            var = jnp.mean(cen * cen, axis=1, keepdims=True)
            hn = cen * lax.rsqrt(var + LN_EPS)
            y = hn * gnw_ref[:, cols] + skip_ref[:, cols] * xc_ref[0, :, cols].astype(F32)
            y = y * _silu(z_ref[0, :, cols].astype(F32))
            part = jnp.dot(y.astype(BF16), wout_ref[cols, :], preferred_element_type=F32)
            acc = part if acc is None else acc + part
        out = x_ref[0] + gate_ref[0] * acc
        o_ref[0] = _rms(out, fg_ref[...]) if final else out


def _ml_main(q, kt, v, cb, gg, at, gt, xc, z, x, gate, gn_w, skip, w_out, nheads, final_g):
    b, l, e = q.shape
    d = x.shape[-1]
    t = ML_CHUNK
    nc = l // t
    lanes = cb.shape[-1]
    dh = e // nheads
    chunk = lambda i, dd, s: (i, s + dd * (nc - 1 - 2 * s), 0)
    chunk_t = lambda i, dd, s: (i, 0, s + dd * (nc - 1 - 2 * s))
    chunk4 = lambda i, dd, s: (i, s + dd * (nc - 1 - 2 * s), 0, 0)
    late = lambda i, dd, s: (i, (nc - 1) - dd * s, 0)
    vec = lambda i, dd, s: (0, 0)
    final = final_g is not None
    extra_specs = [pl.BlockSpec((1, d), vec)] if final else []
    extra_args = [final_g.reshape(1, d)] if final else []
    return pl.pallas_call(
        functools.partial(_ml_main_kernel, nc=nc, nh=nheads, final=final),
        grid=(b, 2, nc),
        in_specs=[pl.BlockSpec((1, t, e), chunk),
                  pl.BlockSpec((1, e, t), chunk_t),
                  pl.BlockSpec((1, t, e), chunk),
                  pl.BlockSpec((1, t, lanes), chunk),
                  pl.BlockSpec((1, t, lanes), chunk),
                  pl.BlockSpec((1, 1, lanes, t), chunk4),
                  pl.BlockSpec((1, 1, lanes, t), chunk4),
                  pl.BlockSpec((1, t, e), late),
                  pl.BlockSpec((1, t, e), late),
                  pl.BlockSpec((1, t, d), late),
                  pl.BlockSpec((1, 1, d), lambda i, dd, s: (i, 0, 0)),
                  pl.BlockSpec((1, e), vec),
                  pl.BlockSpec((1, e), vec),
                  pl.BlockSpec((e, d), vec)] + extra_specs,
        out_specs=pl.BlockSpec((1, t, d), late),
        out_shape=_sds((b, l, d), F32),
        scratch_shapes=[pltpu.VMEM((nheads, dh, dh), F32),
                        pltpu.VMEM((nheads, 1, dh), F32),
                        pltpu.VMEM((nheads, 1, V7X_LANES), F32),
                        pltpu.VMEM((l, e), F32)],
        compiler_params=_params("arbitrary", "arbitrary", "arbitrary"),
        name="ml_main",
    )(q, kt, v, cb, gg, at, gt, xc, z, x, gate, gn_w.reshape(1, e), skip.reshape(1, e), w_out.astype(BF16),
      *extra_args)


def _ml_layer(x, shift, scale, gate, g, w_in, conv_w, conv_b, w_q, w_k, w_v, w_gates, b_gates,
              gn_w, skip, w_out, final_g):
    assert x.shape[1] % ML_CHUNK == 0
    nheads = b_gates.shape[0] // 4
    xm, z = _ml_pre(x, shift, scale, g, w_in.astype(BF16))
    q, kt, v, xc, gp = _ml_qkv(xm, conv_w, conv_b, w_q, w_k, w_v, w_gates, b_gates)
    cb, gg, at, gt = _ml_gates(gp, nheads)
    return _ml_main(q, kt, v, cb, gg, at, gt, xc, z, x, gate, gn_w, skip, w_out, nheads, final_g)


def _final_norm_kernel(x_ref, g_ref, o_ref):
    o_ref[0] = _rms(x_ref[0], g_ref[...])


def _final_norm(x, g):
    b, l, d = x.shape
    tl = min(TOKEN_TILE, l)
    return pl.pallas_call(
        _final_norm_kernel,
        grid=(b, l // tl),
        in_specs=[pl.BlockSpec((1, tl, d), lambda i, j: (i, j, 0)),
                  pl.BlockSpec((1, d), lambda i, j: (0, 0))],
        out_specs=pl.BlockSpec((1, tl, d), lambda i, j: (i, j, 0)),
        out_shape=_sds((b, l, d), F32),
        compiler_params=_params("arbitrary", "arbitrary"),
        name="final_norm",
    )(x, g.reshape(1, d))


def kernel(x, c, ada_w, ada_b, norm_g, s5_w_in, s5_lam_re, s5_lam_im, s5_log_dt, s5_b_re, s5_b_im, s5_c_re, s5_c_im, s5_d, s5_w_glu, s5_b_glu, s5_w_out, ml_w_in, ml_conv_w, ml_conv_b, ml_w_q, ml_w_k, ml_w_v, ml_w_gates, ml_b_gates, ml_gn_w, ml_skip, ml_w_out, final_g):
    depth = ada_w.shape[0]
    d = x.shape[-1]
    mod = _ada(c, ada_w, ada_b)
    for i in range(depth):
        shift = mod[i, :, None, 0:d]
        scale = mod[i, :, None, d:2 * d]
        gate = mod[i, :, None, 2 * d:3 * d]
        g = norm_g[i].reshape(1, d)
        j = i // 2
        if i % 2 == 0:
            x = _s5_layer(x, shift, scale, gate, g, s5_w_in[j], s5_lam_re[j], s5_lam_im[j], s5_log_dt[j],
                          s5_b_re[j], s5_b_im[j], s5_c_re[j], s5_c_im[j], s5_d[j], s5_w_glu[j],
                          s5_b_glu[j], s5_w_out[j])
        else:
            x = _ml_layer(x, shift, scale, gate, g, ml_w_in[j], ml_conv_w[j], ml_conv_b[j], ml_w_q[j],
                          ml_w_k[j], ml_w_v[j], ml_w_gates[j], ml_b_gates[j], ml_gn_w[j], ml_skip[j],
                          ml_w_out[j], final_g if i == depth - 1 else None)
    return x if depth % 2 == 0 else _final_norm(x, final_g)
```

```python
import functools

import jax
import jax.numpy as jnp
from jax import lax
from jax.experimental import pallas as pl
from jax.experimental.pallas import tpu as pltpu

F32 = jnp.float32
BF16 = jnp.bfloat16
U32 = jnp.uint32
HIGHEST = lax.Precision.HIGHEST

RMS_EPS = 1e-6
LN_EPS = 1e-5
V7X_LANES = 128
V7X_SUBLANES = 8
V7X_MXU_DIM = 256
VMEM_LIMIT_BYTES = 56 * 1024 * 1024
S5_CHUNK = V7X_LANES
S5_TILE = V7X_SUBLANES * S5_CHUNK
S5_SUB = 512
ML_CHUNK = V7X_MXU_DIM
TOKEN_TILE = 512


def _params(*sem):
    return pltpu.CompilerParams(dimension_semantics=sem, vmem_limit_bytes=VMEM_LIMIT_BYTES)


def _sds(shape, dtype):
    return jax.ShapeDtypeStruct(shape, dtype)


def _sigmoid(v):
    return 0.5 * jnp.tanh(0.5 * v) + 0.5


def _silu(v):
    return v * _sigmoid(v)


def _nt_dot(a, b, precision=None):
    return lax.dot_general(a, b, (((1,), (1,)), ((), ())), preferred_element_type=F32,
                           precision=precision)


def _cat(parts, axis):
    return parts[0] if len(parts) == 1 else jnp.concatenate(parts, axis=axis)


def _rms(x, g):
    ms = jnp.mean(x * x, axis=-1, keepdims=True)
    return x * lax.rsqrt(ms + RMS_EPS) * g


def _modulated_norm(x, g, shift, scale):
    return _rms(x, g) * (1.0 + scale) + shift


def _ada_kernel(c_ref, w_ref, b_ref, o_ref):
    sc = _silu(c_ref[...])
    o_ref[0] = jnp.dot(sc, w_ref[0], preferred_element_type=F32, precision=HIGHEST) + b_ref[0]


def _ada(c, ada_w, ada_b):
    depth, d, d3 = ada_w.shape
    b = c.shape[0]
    return pl.pallas_call(
        _ada_kernel,
        grid=(depth, d3 // d),
        in_specs=[pl.BlockSpec((b, d), lambda i, j: (0, 0)),
                  pl.BlockSpec((1, d, d), lambda i, j: (i, 0, j)),
                  pl.BlockSpec((1, 1, d), lambda i, j: (i, 0, j))],
        out_specs=pl.BlockSpec((1, b, d), lambda i, j: (i, 0, j)),
        out_shape=_sds((depth, b, d3), F32),
        compiler_params=_params("arbitrary", "arbitrary"),
        name="ada_mod",
    )(c, ada_w, ada_b.reshape(depth, 1, d3))


def _relayout_pitch(e):
    pitch = e + V7X_SUBLANES
    assert pitch % (2 * V7X_SUBLANES) == V7X_SUBLANES
    return pitch


def _s5_pre_kernel(x_ref, shift_ref, scale_ref, g_ref, wt_ref, ut_ref, uzt_ref, rl_scr, *, e, t, sub, pitch):
    kk = sub // t
    for s in range(x_ref.shape[1] // sub):
        h = _modulated_norm(x_ref[0, s * sub:(s + 1) * sub, :], g_ref[...], shift_ref[0], scale_ref[0])
        uz = _nt_dot(wt_ref[...], h.astype(BF16))
        uzt_ref[0, :, s * sub:(s + 1) * sub] = uz.astype(BF16)
        for k in range(kk):
            c = s * kk + k
            rl_scr[c * pitch:c * pitch + e, :] = uz[:e, k * t:(k + 1) * t]

    def gather(ch, carry):
        ut_ref[ch] = rl_scr[pl.ds(ch, V7X_SUBLANES, stride=pitch), :]
        return carry

    lax.fori_loop(0, e, gather, 0, unroll=8)


def _s5_pre(x, shift, scale, g, w_in_t):
    b, l, d = x.shape
    e = w_in_t.shape[0] // 2
    t, tl = S5_CHUNK, S5_TILE
    steps = l // tl
    k = tl // t
    pitch = _relayout_pitch(e)
    return pl.pallas_call(
        functools.partial(_s5_pre_kernel, e=e, t=t, sub=min(S5_SUB, tl), pitch=pitch),
        grid=(b, steps),
        in_specs=[pl.BlockSpec((1, tl, d), lambda i, j: (i, j, 0)),
                  pl.BlockSpec((1, 1, d), lambda i, j: (i, 0, 0)),
                  pl.BlockSpec((1, 1, d), lambda i, j: (i, 0, 0)),
                  pl.BlockSpec((1, d), lambda i, j: (0, 0)),
                  pl.BlockSpec((2 * e, d), lambda i, j: (0, 0))],
        out_specs=[pl.BlockSpec((e, k, t), lambda i, j: (0, i * steps + j, 0)),
                   pl.BlockSpec((1, 2 * e, tl), lambda i, j: (i, 0, j))],
        out_shape=[_sds((e, b * l // t, t), F32), _sds((b, 2 * e, l), BF16)],
        scratch_shapes=[pltpu.VMEM((k * pitch, t), F32)],
        compiler_params=_params("arbitrary", "arbitrary"),
        name="s5_pre",
    )(x, shift, scale, g, w_in_t)


def _abar(lre, lim, ldt):
    dt = jnp.exp(ldt)
    mag = jnp.exp(lre * dt)
    are = mag * jnp.cos(lim * dt)
    aim = mag * jnp.sin(lim * dt)
    den = lre * lre + lim * lim
    zre = ((are - 1.0) * lre + aim * lim) / den
    zim = (aim * lre - (are - 1.0) * lim) / den
    return are, aim, zre, zim


def _cplx_pow(ar, ai, expo, nbits):
    shape = expo.shape
    rr = jnp.ones(shape, F32)
    ri = jnp.zeros(shape, F32)
    pr = jnp.broadcast_to(ar, shape)
    pi = jnp.broadcast_to(ai, shape)
    for k in range(nbits):
        bit = ((expo >> k) & 1) == 1
        mr = jnp.where(bit, pr, 1.0)
        mi = jnp.where(bit, pi, 0.0)
        rr, ri = rr * mr - ri * mi, rr * mi + ri * mr
        if k + 1 < nbits:
            pr, pi = pr * pr - pi * pi, 2.0 * pr * pi
    return rr, ri


def _s5_tables(prm, xmat_scr, ymat_scr, adec_scr, kw_scr, *, t, n, p):
    lre, lim, ldt, btre, btim, cre, cim = prm
    n2 = 2 * n
    nbits = (t - 1).bit_length()
    lane_row = lax.broadcasted_iota(jnp.int32, (1, n2), 1)
    lane_tn = lax.broadcasted_iota(jnp.int32, (t, n2), 1)
    sub_tn = lax.broadcasted_iota(jnp.int32, (t, n2), 0)
    lane_p = lax.broadcasted_iota(jnp.int32, (p, n2), 1)
    lane_pp = lax.broadcasted_iota(jnp.int32, (p * p, n2), 1)
    first_row = lane_row < n
    first_tn = lane_tn < n

    k_halves = []
    k0_bwd = None
    for d in range(2):
        are, aim, zre, zim = _abar(lre[d], lim[d], ldt[d])

        pr, pi = are, aim
        for _ in range(t.bit_length() - 1):
            pr, pi = pr * pr - pi * pi, 2.0 * pr * pi
        adec_scr[2 * d:2 * d + 1, :] = pr
        adec_scr[2 * d + 1:2 * d + 2, :] = jnp.where(first_row, -pi, pi)

        bbt_re = zre * btre[d] - zim * btim[d]
        bbt_im = zre * btim[d] + zim * btre[d]

        asc_re, asc_im = _cplx_pow(are, aim, sub_tn, nbits)
        dsc_re, dsc_im = _cplx_pow(are, aim, t - 1 - sub_tn, nbits)
        up_re, up_im = (asc_re, asc_im) if d == 0 else (dsc_re, dsc_im)
        up_re, up_im = up_re * are - up_im * aim, up_re * aim + up_im * are

        rows = []
        for pp in range(p):
            c_re = cre[d][pp:pp + 1, :]
            c_im = cim[d][pp:pp + 1, :]
            cb_re = c_re * bbt_re - c_im * bbt_im
            cb_im = c_re * bbt_im + c_im * bbt_re
            rows.append(jnp.where(lane_p < n, cb_re, -cb_im))
        cbw = jnp.concatenate(rows, axis=0)
        lag_re, lag_im = (asc_re, asc_im) if d == 0 else (up_re, up_im)
        k_halves.append(_nt_dot(cbw, jnp.where(first_tn, lag_re, lag_im), precision=HIGHEST))
        if d == 1:
            k0_bwd = jnp.sum(jnp.where(lane_pp < n, cbw, 0.0), axis=1, keepdims=True)

        xt_re, xt_im = (dsc_re, dsc_im) if d == 0 else (asc_re, asc_im)
        for q in range(p):
            b_re = bbt_re[q:q + 1, :]
            b_im = bbt_im[q:q + 1, :]
            v1 = jnp.where(first_row, b_re, b_im)
            v2 = jnp.where(first_row, b_im, -b_re)
            xmat_scr[q * t:(q + 1) * t, d * n2:(d + 1) * n2] = (xt_re * v1 - xt_im * v2).astype(BF16)

        ya = jnp.where(first_tn, up_re, -up_im)
        yb = jnp.where(first_tn, up_im, up_re)
        for pp in range(p):
            ymat_scr[pp * t:(pp + 1) * t, d * n2:(d + 1) * n2] = (
                cre[d][pp:pp + 1, :] * ya - cim[d][pp:pp + 1, :] * yb).astype(BF16)

    lane_k = lax.broadcasted_iota(jnp.int32, (p * p, t), 1)
    kfull = jnp.concatenate([k_halves[0] + jnp.where(lane_k == 0, k0_bwd, 0.0), k_halves[1]], axis=1)
    kprev = pltpu.roll(kfull, 1, 1)
    lo = pltpu.bitcast(kfull.astype(BF16).astype(F32), U32) >> 16
    hi = pltpu.bitcast(kprev.astype(BF16).astype(F32), U32) & jnp.uint32(0xFFFF0000)
    words = hi | lo
    kw_scr[0] = words[:, :t]
    kw_scr[1] = words[:, t:]


def _s5_wrap_mask(t):
    wr = lax.broadcasted_iota(jnp.int32, (t // 2, t), 0)
    ln = lax.broadcasted_iota(jnp.int32, (t // 2, t), 1)
    return ln + 2 * wr <= t - 1


def _s5_toep_tile(kw_scr, toep_scr, q, pp, wrap_mask, *, t, p):
    r = pp * p + q
    full = jnp.where(wrap_mask, kw_scr[0, pl.ds(r, 1), :], kw_scr[1, pl.ds(r, 1), :])
    rolled = pltpu.roll(full, 0, 1, stride=2, stride_axis=0)
    start = q * t if isinstance(q, int) else pl.multiple_of(q * t, t)
    toep_scr[pl.ds(start, t), pp * t:(pp + 1) * t] = pltpu.bitcast(rolled, BF16)


def _s5_ssm_kernel(u_ref, *refs, nc, p, t, n):
    cur_prm, nxt_prm, y_ref = refs[0:7], refs[7:14], refs[14]
    toep_scr, xmat_scr, ymat_scr, adec_scr, kw_scr, lhs_scr, xs_scr, xw_scr, sin_scr = refs[15:]
    n2 = 2 * n
    bsz = u_ref.shape[1] // nc
    g = pl.program_id(0)
    slot = lax.rem(g, 2)
    nslot = 1 - slot
    kw = dict(t=t, n=n, p=p)
    wrap_mask = _s5_wrap_mask(t)

    @pl.when(g == 0)
    def _():
        _s5_tables(cur_prm, xmat_scr.at[0], ymat_scr.at[0], adec_scr.at[0], kw_scr.at[0], **kw)

        def q_body(q, carry):
            for pp in range(p):
                _s5_toep_tile(kw_scr.at[0], toep_scr.at[0], q, pp, wrap_mask, t=t, p=p)
            return carry

        lax.fori_loop(0, p, q_body, 0)

    _s5_tables(nxt_prm, xmat_scr.at[nslot], ymat_scr.at[nslot], adec_scr.at[nslot], kw_scr.at[nslot], **kw)

    for q in range(p):
        lhs_scr[:, q * t:(q + 1) * t] = u_ref[q].astype(BF16)
    lhs = lhs_scr[...]

    xmat = xmat_scr.at[slot]
    ymat = ymat_scr.at[slot]
    adec = adec_scr.at[slot]
    toep = toep_scr.at[slot]
    for d in range(2):
        xs = jnp.dot(lhs, xmat[:, d * n2:(d + 1) * n2], preferred_element_type=F32)
        xs_scr[d] = xs
        xw_scr[d] = pltpu.roll(xs, n, 1)

    for d in range(2):
        a1 = adec[2 * d:2 * d + 1, :]
        a2 = adec[2 * d + 1:2 * d + 2, :]
        s = jnp.zeros((bsz, n2), F32)
        sw = jnp.zeros((bsz, n2), F32)
        for c in (range(nc) if d == 0 else reversed(range(nc))):
            rows = pl.ds(c, bsz, stride=nc)
            sin_scr[d, rows, :] = s
            s, sw = (a1 * s + a2 * sw + xs_scr[d, rows, :],
                     a1 * sw - a2 * s + xw_scr[d, rows, :])

    sin_f = sin_scr[0].astype(BF16)
    sin_b = sin_scr[1].astype(BF16)
    w = V7X_MXU_DIM
    npanels = p * t // w
    tiles = [(q, pp) for q in range(p) for pp in range(p)]
    per_panel = len(tiles) // npanels
    for pn in range(npanels):
        cols = slice(pn * w, (pn + 1) * w)
        acc = jnp.dot(lhs, toep[:, cols], preferred_element_type=F32)
        acc = acc + _nt_dot(sin_f, ymat[cols, 0:n2])
        acc = acc + _nt_dot(sin_b, ymat[cols, n2:2 * n2])
        for h in range(w // t):
            y_ref[pn * (w // t) + h] = acc[:, h * t:(h + 1) * t]
        for q, pp in tiles[pn * per_panel:(pn + 1) * per_panel]:
            _s5_toep_tile(kw_scr.at[nslot], toep_scr.at[nslot], q, pp, wrap_mask, t=t, p=p)


def _s5_ssm(ut, nc, lam_re, lam_im, log_dt, b_re, b_im, c_re, c_im):
    e, bc, t = ut.shape
    _, g, n, p = b_re.shape
    n2 = 2 * n
    pt = p * t
    dup = lambda a: jnp.concatenate([a, a], axis=-1)
    lre = dup(lam_re)[:, :, None, :]
    lim = dup(lam_im)[:, :, None, :]
    ldt = log_dt[:, :, None, None]
    btre = dup(jnp.swapaxes(b_re, -1, -2))
    btim = dup(jnp.swapaxes(b_im, -1, -2))
    cre = dup(c_re)
    cim = dup(c_im)
    prm = (lre, lim, ldt, btre, btim, cre, cim)

    def this_group(a):
        return pl.BlockSpec((2, None) + a.shape[2:], lambda i: (0, i, 0, 0))

    def next_group(a):
        return pl.BlockSpec((2, None) + a.shape[2:], lambda i: (0, jnp.minimum(i + 1, g - 1), 0, 0))

    return pl.pallas_call(
        functools.partial(_s5_ssm_kernel, nc=nc, p=p, t=t, n=n),
        grid=(g,),
        in_specs=([pl.BlockSpec((p, bc, t), lambda i: (i, 0, 0))]
                  + [this_group(a) for a in prm] + [next_group(a) for a in prm]),
        out_specs=pl.BlockSpec((p, bc, t), lambda i: (i, 0, 0)),
        out_shape=_sds((e, bc, t), F32),
        scratch_shapes=[pltpu.VMEM((2, pt, pt), BF16),
                        pltpu.VMEM((2, pt, 2 * n2), BF16),
                        pltpu.VMEM((2, pt, 2 * n2), BF16),
                        pltpu.VMEM((2, 4, n2), F32),
                        pltpu.VMEM((2, 2, p * p, t), U32),
                        pltpu.VMEM((bc, pt), BF16),
                        pltpu.VMEM((2, bc, n2), F32),
                        pltpu.VMEM((2, bc, n2), F32),
                        pltpu.VMEM((2, bc, n2), F32)],
        compiler_params=_params("arbitrary"),
        name="s5_ssm",
    )(ut, *prm, *prm)


def _s5_post_kernel(y_ref, uzt_ref, x_ref, gate_ref, dcol_ref, bcol_ref, wgt_ref, wot_ref,
                    o_ref, rl_scr, *, e, t, sub, pitch):
    def scatter(ch, carry):
        rl_scr[pl.ds(ch, V7X_SUBLANES, stride=pitch), :] = y_ref[ch]
        return carry

    lax.fori_loop(0, e, scatter, 0, unroll=8)

    k = sub // t
    reps = sub // dcol_ref.shape[1]
    dfull = _cat([dcol_ref[...]] * reps, 1)
    bfull = _cat([bcol_ref[...]] * reps, 1)
    for s in range(x_ref.shape[1] // sub):
        tok = slice(s * sub, (s + 1) * sub)
        yt = _cat([rl_scr[(s * k + kk) * pitch:(s * k + kk) * pitch + e, :] for kk in range(k)], 1)
        ut = uzt_ref[0, :e, tok].astype(F32)
        y = jax.nn.gelu(yt + dfull * ut)
        g = jnp.dot(wgt_ref[...], y.astype(BF16), preferred_element_type=F32) + bfull
        y = y * _sigmoid(g)
        z = uzt_ref[0, e:, tok].astype(F32)
        yz = y * _silu(z)
        ot = jnp.dot(wot_ref[...], yz.astype(BF16), preferred_element_type=F32)
        o_ref[0, tok, :] = x_ref[0, tok, :] + gate_ref[0] * ot.T


def _s5_post(yt, uzt, x, gate, d_skip, b_glu, w_glu_t, w_out_t):
    b, l, d = x.shape
    e, bc, t = yt.shape
    tl = S5_TILE
    k = tl // t
    lanes = V7X_LANES
    dcol = jnp.broadcast_to(d_skip[:, None], (e, lanes))
    bcol = jnp.broadcast_to(b_glu[:, None], (e, lanes))
    steps = l // tl
    pitch = _relayout_pitch(e)
    return pl.pallas_call(
        functools.partial(_s5_post_kernel, e=e, t=t, sub=min(S5_SUB, tl), pitch=pitch),
        grid=(b, steps),
        in_specs=[pl.BlockSpec((e, k, t), lambda i, j: (0, i * steps + j, 0)),
                  pl.BlockSpec((1, 2 * e, tl), lambda i, j: (i, 0, j)),
                  pl.BlockSpec((1, tl, d), lambda i, j: (i, j, 0)),
                  pl.BlockSpec((1, 1, d), lambda i, j: (i, 0, 0)),
                  pl.BlockSpec((e, lanes), lambda i, j: (0, 0)),
                  pl.BlockSpec((e, lanes), lambda i, j: (0, 0)),
                  pl.BlockSpec((e, e), lambda i, j: (0, 0)),
                  pl.BlockSpec((d, e), lambda i, j: (0, 0))],
        out_specs=pl.BlockSpec((1, tl, d), lambda i, j: (i, j, 0)),
        out_shape=_sds((b, l, d), F32),
        scratch_shapes=[pltpu.VMEM((k * pitch, t), F32)],
        compiler_params=_params("arbitrary", "arbitrary"),
        name="s5_post",
    )(yt, uzt, x, gate, dcol, bcol, w_glu_t, w_out_t)


def _s5_layer(x, shift, scale, gate, g, w_in, lam_re, lam_im, log_dt, b_re, b_im, c_re, c_im,
              d_skip, w_glu, b_glu, w_out):
    l = x.shape[1]
    assert l % S5_TILE == 0
    ut, uzt = _s5_pre(x, shift, scale, g, w_in.T.astype(BF16))
    yt = _s5_ssm(ut, l // S5_CHUNK, lam_re, lam_im, log_dt, b_re, b_im, c_re, c_im)
    return _s5_post(yt, uzt, x, gate, d_skip, b_glu, w_glu.T.astype(BF16), w_out.T.astype(BF16))


def _ml_pre_kernel(x_ref, shift_ref, scale_ref, g_ref, w_ref, xm_ref, z_ref, *, e):
    h = _modulated_norm(x_ref[0], g_ref[...], shift_ref[0], scale_ref[0])
    xz = jnp.dot(h.astype(BF16), w_ref[...], preferred_element_type=F32)
    xm_ref[0] = xz[:, :e]
    z_ref[0] = xz[:, e:].astype(BF16)


def _ml_pre(x, shift, scale, g, w_in):
    b, l, d = x.shape
    e = w_in.shape[1] // 2
    tl = min(TOKEN_TILE, l)
    return pl.pallas_call(
        functools.partial(_ml_pre_kernel, e=e),
        grid=(b, l // tl),
        in_specs=[pl.BlockSpec((1, tl, d), lambda i, j: (i, j, 0)),
                  pl.BlockSpec((1, 1, d), lambda i, j: (i, 0, 0)),
                  pl.BlockSpec((1, 1, d), lambda i, j: (i, 0, 0)),
                  pl.BlockSpec((1, d), lambda i, j: (0, 0)),
                  pl.BlockSpec((d, 2 * e), lambda i, j: (0, 0))],
        out_specs=[pl.BlockSpec((1, tl, e), lambda i, j: (i, j, 0)),
                   pl.BlockSpec((1, tl, e), lambda i, j: (i, j, 0))],
        out_shape=[_sds((b, l, e), F32), _sds((b, l, e), BF16)],
        compiler_params=_params("arbitrary", "arbitrary"),
        name="ml_pre",
    )(x, shift, scale, g, w_in)


HALO = V7X_SUBLANES


def _ml_qkv_kernel(xm_ref, prev_ref, next_ref, cw_ref, cb_ref, wq_ref, wk_ref, wv_ref, wg_ref,
                   bg_ref, q_ref, kt_ref, v_ref, xc_ref, gp_ref, *, steps, taps):
    j = pl.program_id(1)
    xm = xm_ref[0]
    tq, e = xm.shape
    prev = jnp.where(j == 0, 0.0, prev_ref[0])
    nxt = jnp.where(j == steps - 1, 0.0, next_ref[0])
    xe = jnp.concatenate([prev, xm, nxt], axis=0)
    pad = taps // 2
    acc = jnp.broadcast_to(cb_ref[...], (tq, e))
    for kk in range(taps):
        off = HALO + kk - pad
        acc = acc + xe[off:off + tq, :] * cw_ref[kk:kk + 1, :]
    xc = _silu(acc)
    xc_b = xc.astype(BF16)
    xm_b = xm.astype(BF16)
    tile = wq_ref.shape[-1]
    qs, ks, vs = [], [], []
    for i in range(e // tile):
        cols = slice(i * tile, (i + 1) * tile)
        qs.append(jnp.dot(xc_b[:, cols], wq_ref[i], preferred_element_type=F32))
        ks.append(jnp.dot(xc_b[:, cols], wk_ref[i], preferred_element_type=F32).astype(BF16))
        vs.append(jnp.dot(xm_b[:, cols], wv_ref[i], preferred_element_type=F32))
    q = jnp.concatenate(qs, axis=1).astype(BF16)
    k = jnp.concatenate(ks, axis=1)
    v = jnp.concatenate(vs, axis=1).astype(BF16)
    gp = (jnp.dot(q, wg_ref[0], preferred_element_type=F32)
          + jnp.dot(k, wg_ref[1], preferred_element_type=F32)
          + jnp.dot(v, wg_ref[2], preferred_element_type=F32) + bg_ref[...])
    q_ref[0] = q
    for i in range(e // tile):
        kt_ref[0, i * tile:(i + 1) * tile, :] = ks[i].T
    v_ref[0] = v
    xc_ref[0] = xc_b
    gp_ref[0] = gp


def _blockdiag_tiles(w, tile):
    nblk, bs, _ = w.shape
    per = tile // bs
    w4 = w.reshape(nblk // per, per, bs, bs)
    eye = jnp.eye(per, dtype=w.dtype)
    return jnp.einsum('tirc,ij->tirjc', w4, eye).reshape(nblk // per, tile, tile)


def _ml_qkv(xm, conv_w, conv_b, w_q, w_k, w_v, w_gates, b_gates):
    b, l, e = xm.shape
    tq = min(TOKEN_TILE, l)
    steps = l // tq
    taps = conv_w.shape[0]
    tile = V7X_MXU_DIM
    lanes = V7X_LANES
    ng = w_gates.shape[-1]
    wq = _blockdiag_tiles(w_q, tile).astype(BF16)
    wk = _blockdiag_tiles(w_k, tile).astype(BF16)
    wv = _blockdiag_tiles(w_v, tile).astype(BF16)
    wg = jnp.pad(w_gates, ((0, 0), (0, 0), (0, lanes - ng))).astype(BF16)
    bg = jnp.pad(b_gates, (0, lanes - ng)).reshape(1, lanes)
    hb = tq // HALO
    last = l // HALO - 1
    tok = lambda i, j: (i, j, 0)
    full3 = lambda i, j: (0, 0, 0)
    full2 = lambda i, j: (0, 0)
    return pl.pallas_call(
        functools.partial(_ml_qkv_kernel, steps=steps, taps=taps),
        grid=(b, steps),
        in_specs=[pl.BlockSpec((1, tq, e), tok),
                  pl.BlockSpec((1, HALO, e), lambda i, j: (i, jnp.maximum(j * hb - 1, 0), 0)),
                  pl.BlockSpec((1, HALO, e), lambda i, j: (i, jnp.minimum((j + 1) * hb, last), 0)),
                  pl.BlockSpec((taps, e), full2),
                  pl.BlockSpec((1, e), full2),
                  pl.BlockSpec(wq.shape, full3),
                  pl.BlockSpec(wk.shape, full3),
                  pl.BlockSpec(wv.shape, full3),
                  pl.BlockSpec(wg.shape, full3),
                  pl.BlockSpec((1, lanes), full2)],
        out_specs=[pl.BlockSpec((1, tq, e), tok), pl.BlockSpec((1, e, tq), lambda i, j: (i, 0, j)),
                   pl.BlockSpec((1, tq, e), tok), pl.BlockSpec((1, tq, e), tok),
                   pl.BlockSpec((1, tq, lanes), tok)],
        out_shape=[_sds((b, l, e), BF16), _sds((b, e, l), BF16), _sds((b, l, e), BF16),
                   _sds((b, l, e), BF16), _sds((b, l, lanes), F32)],
        compiler_params=_params("arbitrary", "arbitrary"),
        name="ml_qkv",
    )(xm, xm, xm, conv_w, conv_b.reshape(1, e), wq, wk, wv, wg, bg)


def _ml_gates_kernel(gp_ref, cb_ref, gg_ref, at_ref, gt_ref, *, nf, t):
    lanes = gp_ref.shape[2]
    ri = lax.broadcasted_iota(jnp.int32, (t, t), 0)
    ci = lax.broadcasted_iota(jnp.int32, (t, t), 1)
    lower = jnp.where(ci <= ri, 1.0, 0.0)
    upper = jnp.where(ci >= ri, 1.0, 0.0)
    fwd = lax.broadcasted_iota(jnp.int32, (t, lanes), 1) < 2 * nf
    fwd_row = lax.broadcasted_iota(jnp.int32, (1, lanes), 1) < 2 * nf
    for c in range(gp_ref.shape[1] // t):
        rows = slice(c * t, (c + 1) * t)
        x = gp_ref[0, rows, :]
        lf = jax.nn.log_sigmoid(x)
        ipre = pltpu.roll(x, nf, 1)
        cum_f = jnp.dot(lower, lf, preferred_element_type=F32, precision=HIGHEST)
        cum_b = jnp.dot(upper, lf, preferred_element_type=F32, precision=HIGHEST)
        cum = jnp.where(fwd, cum_f, cum_b)
        tot = jnp.where(fwd_row, cum_f[t - 1:t, :], cum_b[0:1, :])
        gg = tot - cum + ipre
        cb_ref[0, rows, :] = cum
        gg_ref[0, rows, :] = gg
        at_ref[0, c] = (ipre - cum).T
        gt_ref[0, c] = gg.T


def _ml_gates(gp, nheads):
    b, l, lanes = gp.shape
    t = ML_CHUNK
    nc = l // t
    return pl.pallas_call(
        functools.partial(_ml_gates_kernel, nf=nheads, t=t),
        grid=(b,),
        in_specs=[pl.BlockSpec((1, l, lanes), lambda i: (i, 0, 0))],
        out_specs=[pl.BlockSpec((1, l, lanes), lambda i: (i, 0, 0)),
                   pl.BlockSpec((1, l, lanes), lambda i: (i, 0, 0)),
                   pl.BlockSpec((1, nc, lanes, t), lambda i: (i, 0, 0, 0)),
                   pl.BlockSpec((1, nc, lanes, t), lambda i: (i, 0, 0, 0))],
        out_shape=[_sds((b, l, lanes), F32), _sds((b, l, lanes), F32), _sds((b, nc, lanes, t), F32),
                   _sds((b, nc, lanes, t), F32)],
        compiler_params=_params("arbitrary"),
        name="ml_gates",
    )(gp)


def _ml_main_kernel(q_ref, kt_ref, v_ref, cb_ref, gg_ref, at_ref, gt_ref, xc_ref, z_ref, x_ref, gate_ref,
                    gnw_ref, skip_ref, wout_ref, *rest, nc, nh, final):
    if final:
        fg_ref, o_ref, c_scr, n_scr, m_scr, hf_scr = rest
    else:
        o_ref, c_scr, n_scr, m_scr, hf_scr = rest
    dirn = pl.program_id(1)
    step = pl.program_id(2)
    t = q_ref.shape[1]
    dh = q_ref.shape[2] // nh
    scale = float(dh) ** -0.5

    @pl.when(step == 0)
    def _():
        c_scr[...] = jnp.zeros_like(c_scr)
        n_scr[...] = jnp.zeros_like(n_scr)
        m_scr[...] = jnp.zeros_like(m_scr)

    ri = lax.broadcasted_iota(jnp.int32, (t, t), 0)
    ci = lax.broadcasted_iota(jnp.int32, (t, t), 1)
    ones_cols = jnp.ones((t, n_scr.shape[2]), BF16)

    def heads_out(d):
        keep = (ci <= ri) if d == 0 else (ci >= ri)
        stage = []
        for h in range(nh):
            cols = slice(h * dh, (h + 1) * dh)
            gcol = d * 2 * nh + nh + h
            q = q_ref[0, :, cols]
            kt = kt_ref[0, cols, :]
            cb = cb_ref[0, :, gcol:gcol + 1]
            gg = gg_ref[0, :, gcol:gcol + 1]
            a_row = at_ref[0, 0, gcol:gcol + 1, :]
            g_row = gt_ref[0, 0, gcol:gcol + 1, :]
            tot = cb[t - 1:t, :] if d == 0 else cb[0:1, :]
            m = m_scr[h][:, 0:1]
            dmat = jnp.where(keep, cb + a_row, -jnp.inf)
            inter = cb + m
            m_t = jnp.maximum(inter, jnp.max(dmat, axis=1, keepdims=True))
            m_new = jnp.maximum(tot + m, jnp.max(gg, axis=0, keepdims=True))
            stage.append(dict(
                q=q, kt=kt, m_t=m_t, m_new=m_new,
                w_intra=jnp.exp(dmat - m_t),
                w_inter=jnp.exp(inter - m_t),
                qk=jnp.dot(q, kt, preferred_element_type=F32),
                decay=jnp.exp(tot + m - m_new),
                wk_row=jnp.exp(g_row - m_new) * scale))
        for h in range(nh):
            cols = slice(h * dh, (h + 1) * dh)
            a = stage[h]
            v = v_ref[0, :, cols]
            s = a['qk'] * scale * a['w_intra']
            qw = a['q'] * a['w_inter'].astype(BF16)
            num = jnp.dot(s.astype(BF16), v, preferred_element_type=F32)
            a['num'] = num + jnp.dot(qw, c_scr[h].astype(BF16), preferred_element_type=F32)
            qn = jnp.dot(a['q'], n_scr[h].astype(BF16), preferred_element_type=F32)[:, 0:1]
            a['nq'] = jnp.sum(s, axis=1, keepdims=True) + a['w_inter'] * qn
        for h in range(nh):
            cols = slice(h * dh, (h + 1) * dh)
            a = stage[h]
            v = v_ref[0, :, cols]
            a['hout'] = a['num'] / jnp.maximum(jnp.abs(a['nq']), jnp.exp(-a['m_t']))
            kwt = a['kt'] * a['wk_row'].astype(BF16)
            a['c_new'] = a['decay'] * c_scr[h] + jnp.dot(kwt, v, preferred_element_type=F32)
            a['n_add'] = jnp.dot(kwt, ones_cols, preferred_element_type=F32)
        for h in range(nh):
            a = stage[h]
            c_scr[h] = a['c_new']
            n_scr[h] = a['decay'] * n_scr[h] + a['n_add']
            m_scr[h] = jnp.broadcast_to(a['m_new'], m_scr.shape[1:])
        return [a['hout'] for a in stage]

    @pl.when(dirn == 0)
    def _():
        rows = pl.ds(pl.multiple_of(step * t, t), t)
        for h, hout in enumerate(heads_out(0)):
            hf_scr[rows, h * dh:(h + 1) * dh] = hout

    @pl.when(dirn == 1)
    def _():
        rows = pl.ds(pl.multiple_of((nc - 1 - step) * t, t), t)
        acc = None
        for h, hout in enumerate(heads_out(1)):
            cols = slice(h * dh, (h + 1) * dh)
            hs = hf_scr[rows, cols] + hout
            mu = jnp.mean(hs, axis=1, keepdims=True)
            cen = hs - mu
            var = jnp.mean(cen * cen, axis=1, keepdims=True)
            hn = cen * lax.rsqrt(var + LN_EPS)
            y = hn * gnw_ref[:, cols] + skip_ref[:, cols] * xc_ref[0, :, cols].astype(F32)
            y = y * _silu(z_ref[0, :, cols].astype(F32))
            part = jnp.dot(y.astype(BF16), wout_ref[cols, :], preferred_element_type=F32)
            acc = part if acc is None else acc + part
        out = x_ref[0] + gate_ref[0] * acc
        o_ref[0] = _rms(out, fg_ref[...]) if final else out


def _ml_main(q, kt, v, cb, gg, at, gt, xc, z, x, gate, gn_w, skip, w_out, nheads, final_g):
    b, l, e = q.shape
    d = x.shape[-1]
    t = ML_CHUNK
    nc = l // t
    lanes = cb.shape[-1]
    dh = e // nheads
    chunk = lambda i, dd, s: (i, s + dd * (nc - 1 - 2 * s), 0)
    chunk_t = lambda i, dd, s: (i, 0, s + dd * (nc - 1 - 2 * s))
    chunk4 = lambda i, dd, s: (i, s + dd * (nc - 1 - 2 * s), 0, 0)
    late = lambda i, dd, s: (i, (nc - 1) - dd * s, 0)
    vec = lambda i, dd, s: (0, 0)
    final = final_g is not None
    extra_specs = [pl.BlockSpec((1, d), vec)] if final else []
    extra_args = [final_g.reshape(1, d)] if final else []
    return pl.pallas_call(
        functools.partial(_ml_main_kernel, nc=nc, nh=nheads, final=final),
        grid=(b, 2, nc),
        in_specs=[pl.BlockSpec((1, t, e), chunk),
                  pl.BlockSpec((1, e, t), chunk_t),
                  pl.BlockSpec((1, t, e), chunk),
                  pl.BlockSpec((1, t, lanes), chunk),
                  pl.BlockSpec((1, t, lanes), chunk),
                  pl.BlockSpec((1, 1, lanes, t), chunk4),
                  pl.BlockSpec((1, 1, lanes, t), chunk4),
                  pl.BlockSpec((1, t, e), late),
                  pl.BlockSpec((1, t, e), late),
                  pl.BlockSpec((1, t, d), late),
                  pl.BlockSpec((1, 1, d), lambda i, dd, s: (i, 0, 0)),
                  pl.BlockSpec((1, e), vec),
                  pl.BlockSpec((1, e), vec),
                  pl.BlockSpec((e, d), vec)] + extra_specs,
        out_specs=pl.BlockSpec((1, t, d), late),
        out_shape=_sds((b, l, d), F32),
        scratch_shapes=[pltpu.VMEM((nheads, dh, dh), F32),
                        pltpu.VMEM((nheads, dh, V7X_LANES), F32),
                        pltpu.VMEM((nheads, 1, V7X_LANES), F32),
                        pltpu.VMEM((l, e), F32)],
        compiler_params=_params("arbitrary", "arbitrary", "arbitrary"),
        name="ml_main",
    )(q, kt, v, cb, gg, at, gt, xc, z, x, gate, gn_w.reshape(1, e), skip.reshape(1, e), w_out.astype(BF16),
      *extra_args)


def _ml_layer(x, shift, scale, gate, g, w_in, conv_w, conv_b, w_q, w_k, w_v, w_gates, b_gates,
              gn_w, skip, w_out, final_g):
    assert x.shape[1] % ML_CHUNK == 0
    nheads = b_gates.shape[0] // 4
    xm, z = _ml_pre(x, shift, scale, g, w_in.astype(BF16))
    q, kt, v, xc, gp = _ml_qkv(xm, conv_w, conv_b, w_q, w_k, w_v, w_gates, b_gates)
    cb, gg, at, gt = _ml_gates(gp, nheads)
    return _ml_main(q, kt, v, cb, gg, at, gt, xc, z, x, gate, gn_w, skip, w_out, nheads, final_g)


def _final_norm_kernel(x_ref, g_ref, o_ref):
    o_ref[0] = _rms(x_ref[0], g_ref[...])


def _final_norm(x, g):
    b, l, d = x.shape
    tl = min(TOKEN_TILE, l)
    return pl.pallas_call(
        _final_norm_kernel,
        grid=(b, l // tl),
        in_specs=[pl.BlockSpec((1, tl, d), lambda i, j: (i, j, 0)),
                  pl.BlockSpec((1, d), lambda i, j: (0, 0))],
        out_specs=pl.BlockSpec((1, tl, d), lambda i, j: (i, j, 0)),
        out_shape=_sds((b, l, d), F32),
        compiler_params=_params("arbitrary", "arbitrary"),
        name="final_norm",
    )(x, g.reshape(1, d))


def kernel(x, c, ada_w, ada_b, norm_g, s5_w_in, s5_lam_re, s5_lam_im, s5_log_dt, s5_b_re, s5_b_im, s5_c_re, s5_c_im, s5_d, s5_w_glu, s5_b_glu, s5_w_out, ml_w_in, ml_conv_w, ml_conv_b, ml_w_q, ml_w_k, ml_w_v, ml_w_gates, ml_b_gates, ml_gn_w, ml_skip, ml_w_out, final_g):
    depth = ada_w.shape[0]
    d = x.shape[-1]
    mod = _ada(c, ada_w, ada_b)
    for i in range(depth):
        shift = mod[i, :, None, 0:d]
        scale = mod[i, :, None, d:2 * d]
        gate = mod[i, :, None, 2 * d:3 * d]
        g = norm_g[i].reshape(1, d)
        j = i // 2
        if i % 2 == 0:
            x = _s5_layer(x, shift, scale, gate, g, s5_w_in[j], s5_lam_re[j], s5_lam_im[j], s5_log_dt[j],
                          s5_b_re[j], s5_b_im[j], s5_c_re[j], s5_c_im[j], s5_d[j], s5_w_glu[j],
                          s5_b_glu[j], s5_w_out[j])
        else:
            x = _ml_layer(x, shift, scale, gate, g, ml_w_in[j], ml_conv_w[j], ml_conv_b[j], ml_w_q[j],
                          ml_w_k[j], ml_w_v[j], ml_w_gates[j], ml_b_gates[j], ml_gn_w[j], ml_skip[j],
                          ml_w_out[j], final_g if i == depth - 1 else None)
    return x if depth % 2 == 0 else _final_norm(x, final_g)
```

```python
import functools

import jax
import jax.numpy as jnp
from jax import lax
from jax.experimental import pallas as pl
from jax.experimental.pallas import tpu as pltpu

F32 = jnp.float32
BF16 = jnp.bfloat16
U32 = jnp.uint32
HIGHEST = lax.Precision.HIGHEST

RMS_EPS = 1e-6
LN_EPS = 1e-5
V7X_LANES = 128
V7X_SUBLANES = 8
V7X_MXU_DIM = 256
VMEM_LIMIT_BYTES = 56 * 1024 * 1024
S5_CHUNK = V7X_LANES
S5_TILE = V7X_SUBLANES * S5_CHUNK
S5_SUB = 512
ML_CHUNK = V7X_MXU_DIM
TOKEN_TILE = 512


def _params(*sem):
    return pltpu.CompilerParams(dimension_semantics=sem, vmem_limit_bytes=VMEM_LIMIT_BYTES)


def _sds(shape, dtype):
    return jax.ShapeDtypeStruct(shape, dtype)


def _sigmoid(v):
    return 0.5 * jnp.tanh(0.5 * v) + 0.5


def _silu(v):
    return v * _sigmoid(v)


def _nt_dot(a, b, precision=None):
    return lax.dot_general(a, b, (((1,), (1,)), ((), ())), preferred_element_type=F32,
                           precision=precision)


def _cat(parts, axis):
    return parts[0] if len(parts) == 1 else jnp.concatenate(parts, axis=axis)


def _rms(x, g):
    ms = jnp.mean(x * x, axis=-1, keepdims=True)
    return x * lax.rsqrt(ms + RMS_EPS) * g


def _modulated_norm(x, g, shift, scale):
    return _rms(x, g) * (1.0 + scale) + shift


def _ada_kernel(c_ref, w_ref, b_ref, o_ref):
    sc = _silu(c_ref[...])
    o_ref[0] = jnp.dot(sc, w_ref[0], preferred_element_type=F32, precision=HIGHEST) + b_ref[0]


def _ada(c, ada_w, ada_b):
    depth, d, d3 = ada_w.shape
    b = c.shape[0]
    return pl.pallas_call(
        _ada_kernel,
        grid=(depth, d3 // d),
        in_specs=[pl.BlockSpec((b, d), lambda i, j: (0, 0)),
                  pl.BlockSpec((1, d, d), lambda i, j: (i, 0, j)),
                  pl.BlockSpec((1, 1, d), lambda i, j: (i, 0, j))],
        out_specs=pl.BlockSpec((1, b, d), lambda i, j: (i, 0, j)),
        out_shape=_sds((depth, b, d3), F32),
        compiler_params=_params("arbitrary", "arbitrary"),
        name="ada_mod",
    )(c, ada_w, ada_b.reshape(depth, 1, d3))


def _relayout_pitch(e):
    pitch = e + V7X_SUBLANES
    assert pitch % (2 * V7X_SUBLANES) == V7X_SUBLANES
    return pitch


def _s5_pre_kernel(x_ref, shift_ref, scale_ref, g_ref, wt_ref, ut_ref, uzt_ref, rl_scr, *, e, t, sub, pitch):
    kk = sub // t
    for s in range(x_ref.shape[1] // sub):
        h = _modulated_norm(x_ref[0, s * sub:(s + 1) * sub, :], g_ref[...], shift_ref[0], scale_ref[0])
        uz = _nt_dot(wt_ref[...], h.astype(BF16))
        uzt_ref[0, :, s * sub:(s + 1) * sub] = uz.astype(BF16)
        for k in range(kk):
            c = s * kk + k
            rl_scr[c * pitch:c * pitch + e, :] = uz[:e, k * t:(k + 1) * t]

    def gather(ch, carry):
        ut_ref[ch] = rl_scr[pl.ds(ch, V7X_SUBLANES, stride=pitch), :]
        return carry

    lax.fori_loop(0, e, gather, 0, unroll=8)


def _s5_pre(x, shift, scale, g, w_in_t):
    b, l, d = x.shape
    e = w_in_t.shape[0] // 2
    t, tl = S5_CHUNK, S5_TILE
    steps = l // tl
    k = tl // t
    pitch = _relayout_pitch(e)
    return pl.pallas_call(
        functools.partial(_s5_pre_kernel, e=e, t=t, sub=min(S5_SUB, tl), pitch=pitch),
        grid=(b, steps),
        in_specs=[pl.BlockSpec((1, tl, d), lambda i, j: (i, j, 0)),
                  pl.BlockSpec((1, 1, d), lambda i, j: (i, 0, 0)),
                  pl.BlockSpec((1, 1, d), lambda i, j: (i, 0, 0)),
                  pl.BlockSpec((1, d), lambda i, j: (0, 0)),
                  pl.BlockSpec((2 * e, d), lambda i, j: (0, 0))],
        out_specs=[pl.BlockSpec((e, k, t), lambda i, j: (0, i * steps + j, 0)),
                   pl.BlockSpec((1, 2 * e, tl), lambda i, j: (i, 0, j))],
        out_shape=[_sds((e, b * l // t, t), F32), _sds((b, 2 * e, l), BF16)],
        scratch_shapes=[pltpu.VMEM((k * pitch, t), F32)],
        compiler_params=_params("arbitrary", "arbitrary"),
        name="s5_pre",
    )(x, shift, scale, g, w_in_t)


def _abar(lre, lim, ldt):
    dt = jnp.exp(ldt)
    mag = jnp.exp(lre * dt)
    are = mag * jnp.cos(lim * dt)
    aim = mag * jnp.sin(lim * dt)
    den = lre * lre + lim * lim
    zre = ((are - 1.0) * lre + aim * lim) / den
    zim = (aim * lre - (are - 1.0) * lim) / den
    return are, aim, zre, zim


def _cplx_pow(ar, ai, expo, nbits):
    shape = expo.shape
    rr = jnp.ones(shape, F32)
    ri = jnp.zeros(shape, F32)
    pr = jnp.broadcast_to(ar, shape)
    pi = jnp.broadcast_to(ai, shape)
    for k in range(nbits):
        bit = ((expo >> k) & 1) == 1
        mr = jnp.where(bit, pr, 1.0)
        mi = jnp.where(bit, pi, 0.0)
        rr, ri = rr * mr - ri * mi, rr * mi + ri * mr
        if k + 1 < nbits:
            pr, pi = pr * pr - pi * pi, 2.0 * pr * pi
    return rr, ri


def _s5_tables(prm, xmat_scr, ymat_scr, adec_scr, kw_scr, *, t, n, p):
    lre, lim, ldt, btre, btim, cre, cim = prm
    n2 = 2 * n
    nbits = (t - 1).bit_length()
    lane_row = lax.broadcasted_iota(jnp.int32, (1, n2), 1)
    lane_tn = lax.broadcasted_iota(jnp.int32, (t, n2), 1)
    sub_tn = lax.broadcasted_iota(jnp.int32, (t, n2), 0)
    lane_p = lax.broadcasted_iota(jnp.int32, (p, n2), 1)
    lane_pp = lax.broadcasted_iota(jnp.int32, (p * p, n2), 1)
    first_row = lane_row < n
    first_tn = lane_tn < n

    k_halves = []
    k0_bwd = None
    for d in range(2):
        are, aim, zre, zim = _abar(lre[d], lim[d], ldt[d])

        pr, pi = are, aim
        for _ in range(t.bit_length() - 1):
            pr, pi = pr * pr - pi * pi, 2.0 * pr * pi
        adec_scr[2 * d:2 * d + 1, :] = pr
        adec_scr[2 * d + 1:2 * d + 2, :] = jnp.where(first_row, -pi, pi)

        bbt_re = zre * btre[d] - zim * btim[d]
        bbt_im = zre * btim[d] + zim * btre[d]

        asc_re, asc_im = _cplx_pow(are, aim, sub_tn, nbits)
        dsc_re, dsc_im = _cplx_pow(are, aim, t - 1 - sub_tn, nbits)
        up_re, up_im = (asc_re, asc_im) if d == 0 else (dsc_re, dsc_im)
        up_re, up_im = up_re * are - up_im * aim, up_re * aim + up_im * are

        rows = []
        for pp in range(p):
            c_re = cre[d][pp:pp + 1, :]
            c_im = cim[d][pp:pp + 1, :]
            cb_re = c_re * bbt_re - c_im * bbt_im
            cb_im = c_re * bbt_im + c_im * bbt_re
            rows.append(jnp.where(lane_p < n, cb_re, -cb_im))
        cbw = jnp.concatenate(rows, axis=0)
        lag_re, lag_im = (asc_re, asc_im) if d == 0 else (up_re, up_im)
        k_halves.append(_nt_dot(cbw, jnp.where(first_tn, lag_re, lag_im), precision=HIGHEST))
        if d == 1:
            k0_bwd = jnp.sum(jnp.where(lane_pp < n, cbw, 0.0), axis=1, keepdims=True)

        xt_re, xt_im = (dsc_re, dsc_im) if d == 0 else (asc_re, asc_im)
        for q in range(p):
            b_re = bbt_re[q:q + 1, :]
            b_im = bbt_im[q:q + 1, :]
            v1 = jnp.where(first_row, b_re, b_im)
            v2 = jnp.where(first_row, b_im, -b_re)
            xmat_scr[q * t:(q + 1) * t, d * n2:(d + 1) * n2] = (xt_re * v1 - xt_im * v2).astype(BF16)

        ya = jnp.where(first_tn, up_re, -up_im)
        yb = jnp.where(first_tn, up_im, up_re)
        for pp in range(p):
            ymat_scr[pp * t:(pp + 1) * t, d * n2:(d + 1) * n2] = (
                cre[d][pp:pp + 1, :] * ya - cim[d][pp:pp + 1, :] * yb).astype(BF16)

    lane_k = lax.broadcasted_iota(jnp.int32, (p * p, t), 1)
    kfull = jnp.concatenate([k_halves[0] + jnp.where(lane_k == 0, k0_bwd, 0.0), k_halves[1]], axis=1)
    kprev = pltpu.roll(kfull, 1, 1)
    lo = pltpu.bitcast(kfull.astype(BF16).astype(F32), U32) >> 16
    hi = pltpu.bitcast(kprev.astype(BF16).astype(F32), U32) & jnp.uint32(0xFFFF0000)
    words = hi | lo
    kw_scr[0] = words[:, :t]
    kw_scr[1] = words[:, t:]


def _s5_wrap_mask(t):
    wr = lax.broadcasted_iota(jnp.int32, (t // 2, t), 0)
    ln = lax.broadcasted_iota(jnp.int32, (t // 2, t), 1)
    return ln + 2 * wr <= t - 1


def _s5_toep_tile(kw_scr, toep_scr, q, pp, wrap_mask, *, t, p):
    r = pp * p + q
    full = jnp.where(wrap_mask, kw_scr[0, pl.ds(r, 1), :], kw_scr[1, pl.ds(r, 1), :])
    rolled = pltpu.roll(full, 0, 1, stride=2, stride_axis=0)
    start = q * t if isinstance(q, int) else pl.multiple_of(q * t, t)
    toep_scr[pl.ds(start, t), pp * t:(pp + 1) * t] = pltpu.bitcast(rolled, BF16)


def _s5_ssm_kernel(u_ref, *refs, nc, p, t, n):
    cur_prm, nxt_prm, y_ref = refs[0:7], refs[7:14], refs[14]
    toep_scr, xmat_scr, ymat_scr, adec_scr, kw_scr, lhs_scr, xs_scr, xw_scr, sin_scr = refs[15:]
    n2 = 2 * n
    bsz = u_ref.shape[1] // nc
    g = pl.program_id(0)
    slot = lax.rem(g, 2)
    nslot = 1 - slot
    kw = dict(t=t, n=n, p=p)
    wrap_mask = _s5_wrap_mask(t)

    @pl.when(g == 0)
    def _():
        _s5_tables(cur_prm, xmat_scr.at[0], ymat_scr.at[0], adec_scr.at[0], kw_scr.at[0], **kw)

        def q_body(q, carry):
            for pp in range(p):
                _s5_toep_tile(kw_scr.at[0], toep_scr.at[0], q, pp, wrap_mask, t=t, p=p)
            return carry

        lax.fori_loop(0, p, q_body, 0)

    _s5_tables(nxt_prm, xmat_scr.at[nslot], ymat_scr.at[nslot], adec_scr.at[nslot], kw_scr.at[nslot], **kw)

    for q in range(p):
        lhs_scr[:, q * t:(q + 1) * t] = u_ref[q].astype(BF16)
    lhs = lhs_scr[...]

    xmat = xmat_scr.at[slot]
    ymat = ymat_scr.at[slot]
    adec = adec_scr.at[slot]
    toep = toep_scr.at[slot]
    xs_both = jnp.dot(lhs, xmat[...], preferred_element_type=F32)
    for d in range(2):
        xs = xs_both[:, d * n2:(d + 1) * n2]
        xs_scr[d] = xs
        xw_scr[d] = pltpu.roll(xs, n, 1)

    for d in range(2):
        a1 = adec[2 * d:2 * d + 1, :]
        a2 = adec[2 * d + 1:2 * d + 2, :]
        s = jnp.zeros((bsz, n2), F32)
        sw = jnp.zeros((bsz, n2), F32)
        for c in (range(nc) if d == 0 else reversed(range(nc))):
            rows = pl.ds(c, bsz, stride=nc)
            sin_scr[d, rows, :] = s
            s, sw = (a1 * s + a2 * sw + xs_scr[d, rows, :],
                     a1 * sw - a2 * s + xw_scr[d, rows, :])

    sin_both = jnp.concatenate([sin_scr[0].astype(BF16), sin_scr[1].astype(BF16)], axis=1)
    w = V7X_MXU_DIM
    npanels = p * t // w
    tiles = [(q, pp) for q in range(p) for pp in range(p)]
    per_panel = len(tiles) // npanels
    for pn in range(npanels):
        cols = slice(pn * w, (pn + 1) * w)
        acc = jnp.dot(lhs, toep[:, cols], preferred_element_type=F32)
        acc = acc + _nt_dot(sin_both, ymat[cols, :])
        for h in range(w // t):
            y_ref[pn * (w // t) + h] = acc[:, h * t:(h + 1) * t]
        for q, pp in tiles[pn * per_panel:(pn + 1) * per_panel]:
            _s5_toep_tile(kw_scr.at[nslot], toep_scr.at[nslot], q, pp, wrap_mask, t=t, p=p)


def _s5_ssm(ut, nc, lam_re, lam_im, log_dt, b_re, b_im, c_re, c_im):
    e, bc, t = ut.shape
    _, g, n, p = b_re.shape
    n2 = 2 * n
    pt = p * t
    dup = lambda a: jnp.concatenate([a, a], axis=-1)
    lre = dup(lam_re)[:, :, None, :]
    lim = dup(lam_im)[:, :, None, :]
    ldt = log_dt[:, :, None, None]
    btre = dup(jnp.swapaxes(b_re, -1, -2))
    btim = dup(jnp.swapaxes(b_im, -1, -2))
    cre = dup(c_re)
    cim = dup(c_im)
    prm = (lre, lim, ldt, btre, btim, cre, cim)

    def this_group(a):
        return pl.BlockSpec((2, None) + a.shape[2:], lambda i: (0, i, 0, 0))

    def next_group(a):
        return pl.BlockSpec((2, None) + a.shape[2:], lambda i: (0, jnp.minimum(i + 1, g - 1), 0, 0))

    return pl.pallas_call(
        functools.partial(_s5_ssm_kernel, nc=nc, p=p, t=t, n=n),
        grid=(g,),
        in_specs=([pl.BlockSpec((p, bc, t), lambda i: (i, 0, 0))]
                  + [this_group(a) for a in prm] + [next_group(a) for a in prm]),
        out_specs=pl.BlockSpec((p, bc, t), lambda i: (i, 0, 0)),
        out_shape=_sds((e, bc, t), F32),
        scratch_shapes=[pltpu.VMEM((2, pt, pt), BF16),
                        pltpu.VMEM((2, pt, 2 * n2), BF16),
                        pltpu.VMEM((2, pt, 2 * n2), BF16),
                        pltpu.VMEM((2, 4, n2), F32),
                        pltpu.VMEM((2, 2, p * p, t), U32),
                        pltpu.VMEM((bc, pt), BF16),
                        pltpu.VMEM((2, bc, n2), F32),
                        pltpu.VMEM((2, bc, n2), F32),
                        pltpu.VMEM((2, bc, n2), F32)],
        compiler_params=_params("arbitrary"),
        name="s5_ssm",
    )(ut, *prm, *prm)


def _s5_post_kernel(y_ref, uzt_ref, x_ref, gate_ref, dcol_ref, bcol_ref, wgt_ref, wot_ref,
                    o_ref, rl_scr, *, e, t, sub, pitch):
    def scatter(ch, carry):
        rl_scr[pl.ds(ch, V7X_SUBLANES, stride=pitch), :] = y_ref[ch]
        return carry

    lax.fori_loop(0, e, scatter, 0, unroll=8)

    k = sub // t
    reps = sub // dcol_ref.shape[1]
    dfull = _cat([dcol_ref[...]] * reps, 1)
    bfull = _cat([bcol_ref[...]] * reps, 1)
    for s in range(x_ref.shape[1] // sub):
        tok = slice(s * sub, (s + 1) * sub)
        yt = _cat([rl_scr[(s * k + kk) * pitch:(s * k + kk) * pitch + e, :] for kk in range(k)], 1)
        ut = uzt_ref[0, :e, tok].astype(F32)
        y = jax.nn.gelu(yt + dfull * ut)
        g = jnp.dot(wgt_ref[...], y.astype(BF16), preferred_element_type=F32) + bfull
        y = y * _sigmoid(g)
        z = uzt_ref[0, e:, tok].astype(F32)
        yz = y * _silu(z)
        ot = jnp.dot(wot_ref[...], yz.astype(BF16), preferred_element_type=F32)
        o_ref[0, tok, :] = x_ref[0, tok, :] + gate_ref[0] * ot.T


def _s5_post(yt, uzt, x, gate, d_skip, b_glu, w_glu_t, w_out_t):
    b, l, d = x.shape
    e, bc, t = yt.shape
    tl = S5_TILE
    k = tl // t
    lanes = V7X_LANES
    dcol = jnp.broadcast_to(d_skip[:, None], (e, lanes))
    bcol = jnp.broadcast_to(b_glu[:, None], (e, lanes))
    steps = l // tl
    pitch = _relayout_pitch(e)
    return pl.pallas_call(
        functools.partial(_s5_post_kernel, e=e, t=t, sub=min(S5_SUB, tl), pitch=pitch),
        grid=(b, steps),
        in_specs=[pl.BlockSpec((e, k, t), lambda i, j: (0, i * steps + j, 0)),
                  pl.BlockSpec((1, 2 * e, tl), lambda i, j: (i, 0, j)),
                  pl.BlockSpec((1, tl, d), lambda i, j: (i, j, 0)),
                  pl.BlockSpec((1, 1, d), lambda i, j: (i, 0, 0)),
                  pl.BlockSpec((e, lanes), lambda i, j: (0, 0)),
                  pl.BlockSpec((e, lanes), lambda i, j: (0, 0)),
                  pl.BlockSpec((e, e), lambda i, j: (0, 0)),
                  pl.BlockSpec((d, e), lambda i, j: (0, 0))],
        out_specs=pl.BlockSpec((1, tl, d), lambda i, j: (i, j, 0)),
        out_shape=_sds((b, l, d), F32),
        scratch_shapes=[pltpu.VMEM((k * pitch, t), F32)],
        compiler_params=_params("arbitrary", "arbitrary"),
        name="s5_post",
    )(yt, uzt, x, gate, dcol, bcol, w_glu_t, w_out_t)


def _s5_layer(x, shift, scale, gate, g, w_in, lam_re, lam_im, log_dt, b_re, b_im, c_re, c_im,
              d_skip, w_glu, b_glu, w_out):
    l = x.shape[1]
    assert l % S5_TILE == 0
    ut, uzt = _s5_pre(x, shift, scale, g, w_in.T.astype(BF16))
    yt = _s5_ssm(ut, l // S5_CHUNK, lam_re, lam_im, log_dt, b_re, b_im, c_re, c_im)
    return _s5_post(yt, uzt, x, gate, d_skip, b_glu, w_glu.T.astype(BF16), w_out.T.astype(BF16))


def _ml_pre_kernel(x_ref, shift_ref, scale_ref, g_ref, w_ref, xm_ref, z_ref, *, e):
    h = _modulated_norm(x_ref[0], g_ref[...], shift_ref[0], scale_ref[0])
    xz = jnp.dot(h.astype(BF16), w_ref[...], preferred_element_type=F32)
    xm_ref[0] = xz[:, :e].astype(BF16)
    z_ref[0] = xz[:, e:].astype(BF16)


def _ml_pre(x, shift, scale, g, w_in):
    b, l, d = x.shape
    e = w_in.shape[1] // 2
    tl = min(TOKEN_TILE, l)
    return pl.pallas_call(
        functools.partial(_ml_pre_kernel, e=e),
        grid=(b, l // tl),
        in_specs=[pl.BlockSpec((1, tl, d), lambda i, j: (i, j, 0)),
                  pl.BlockSpec((1, 1, d), lambda i, j: (i, 0, 0)),
                  pl.BlockSpec((1, 1, d), lambda i, j: (i, 0, 0)),
                  pl.BlockSpec((1, d), lambda i, j: (0, 0)),
                  pl.BlockSpec((d, 2 * e), lambda i, j: (0, 0))],
        out_specs=[pl.BlockSpec((1, tl, e), lambda i, j: (i, j, 0)),
                   pl.BlockSpec((1, tl, e), lambda i, j: (i, j, 0))],
        out_shape=[_sds((b, l, e), BF16), _sds((b, l, e), BF16)],
        compiler_params=_params("arbitrary", "arbitrary"),
        name="ml_pre",
    )(x, shift, scale, g, w_in)


HALO = 2 * V7X_SUBLANES


def _ml_qkv_kernel(xm_ref, prev_ref, next_ref, cw_ref, cb_ref, wq_ref, wk_ref, wv_ref, wg_ref,
                   bg_ref, q_ref, kt_ref, v_ref, xc_ref, gp_ref, *, steps, taps):
    j = pl.program_id(1)
    xm_b = xm_ref[0]
    tq, e = xm_b.shape
    zero = jnp.zeros((), BF16)
    prev = jnp.where(j == 0, zero, prev_ref[0])
    nxt = jnp.where(j == steps - 1, zero, next_ref[0])
    xe = jnp.concatenate([prev, xm_b, nxt], axis=0).astype(F32)
    pad = taps // 2
    acc = jnp.broadcast_to(cb_ref[...], (tq, e))
    for kk in range(taps):
        off = HALO + kk - pad
        acc = acc + xe[off:off + tq, :] * cw_ref[kk:kk + 1, :]
    xc = _silu(acc)
    xc_b = xc.astype(BF16)
    tile = wq_ref.shape[-1]
    qs, ks, vs = [], [], []
    for i in range(e // tile):
        cols = slice(i * tile, (i + 1) * tile)
        qs.append(jnp.dot(xc_b[:, cols], wq_ref[i], preferred_element_type=F32))
        ks.append(jnp.dot(xc_b[:, cols], wk_ref[i], preferred_element_type=F32).astype(BF16))
        vs.append(jnp.dot(xm_b[:, cols], wv_ref[i], preferred_element_type=F32))
    q = jnp.concatenate(qs, axis=1).astype(BF16)
    k = jnp.concatenate(ks, axis=1)
    v = jnp.concatenate(vs, axis=1).astype(BF16)
    gp = (jnp.dot(q, wg_ref[0], preferred_element_type=F32)
          + jnp.dot(k, wg_ref[1], preferred_element_type=F32)
          + jnp.dot(v, wg_ref[2], preferred_element_type=F32) + bg_ref[...])
    q_ref[0] = q
    for i in range(e // tile):
        kt_ref[0, i * tile:(i + 1) * tile, :] = ks[i].T
    v_ref[0] = v
    xc_ref[0] = xc_b
    gp_ref[0] = gp


def _blockdiag_tiles(w, tile):
    nblk, bs, _ = w.shape
    per = tile // bs
    w4 = w.reshape(nblk // per, per, bs, bs)
    eye = jnp.eye(per, dtype=w.dtype)
    return jnp.einsum('tirc,ij->tirjc', w4, eye).reshape(nblk // per, tile, tile)


def _ml_qkv(xm, conv_w, conv_b, w_q, w_k, w_v, w_gates, b_gates):
    b, l, e = xm.shape
    tq = min(TOKEN_TILE, l)
    steps = l // tq
    taps = conv_w.shape[0]
    tile = V7X_MXU_DIM
    lanes = V7X_LANES
    ng = w_gates.shape[-1]
    wq = _blockdiag_tiles(w_q, tile).astype(BF16)
    wk = _blockdiag_tiles(w_k, tile).astype(BF16)
    wv = _blockdiag_tiles(w_v, tile).astype(BF16)
    wg = jnp.pad(w_gates, ((0, 0), (0, 0), (0, lanes - ng))).astype(BF16)
    bg = jnp.pad(b_gates, (0, lanes - ng)).reshape(1, lanes)
    hb = tq // HALO
    last = l // HALO - 1
    tok = lambda i, j: (i, j, 0)
    full3 = lambda i, j: (0, 0, 0)
    full2 = lambda i, j: (0, 0)
    return pl.pallas_call(
        functools.partial(_ml_qkv_kernel, steps=steps, taps=taps),
        grid=(b, steps),
        in_specs=[pl.BlockSpec((1, tq, e), tok),
                  pl.BlockSpec((1, HALO, e), lambda i, j: (i, jnp.maximum(j * hb - 1, 0), 0)),
                  pl.BlockSpec((1, HALO, e), lambda i, j: (i, jnp.minimum((j + 1) * hb, last), 0)),
                  pl.BlockSpec((taps, e), full2),
                  pl.BlockSpec((1, e), full2),
                  pl.BlockSpec(wq.shape, full3),
                  pl.BlockSpec(wk.shape, full3),
                  pl.BlockSpec(wv.shape, full3),
                  pl.BlockSpec(wg.shape, full3),
                  pl.BlockSpec((1, lanes), full2)],
        out_specs=[pl.BlockSpec((1, tq, e), tok), pl.BlockSpec((1, e, tq), lambda i, j: (i, 0, j)),
                   pl.BlockSpec((1, tq, e), tok), pl.BlockSpec((1, tq, e), tok),
                   pl.BlockSpec((1, tq, lanes), tok)],
        out_shape=[_sds((b, l, e), BF16), _sds((b, e, l), BF16), _sds((b, l, e), BF16),
                   _sds((b, l, e), BF16), _sds((b, l, lanes), F32)],
        compiler_params=_params("arbitrary", "arbitrary"),
        name="ml_qkv",
    )(xm, xm, xm, conv_w, conv_b.reshape(1, e), wq, wk, wv, wg, bg)


def _ml_gates_kernel(gp_ref, cb_ref, gg_ref, at_ref, gt_ref, *, nf, t):
    lanes = gp_ref.shape[2]
    ri = lax.broadcasted_iota(jnp.int32, (t, t), 0)
    ci = lax.broadcasted_iota(jnp.int32, (t, t), 1)
    lower = jnp.where(ci <= ri, 1.0, 0.0)
    upper = jnp.where(ci >= ri, 1.0, 0.0)
    fwd = lax.broadcasted_iota(jnp.int32, (t, lanes), 1) < 2 * nf
    fwd_row = lax.broadcasted_iota(jnp.int32, (1, lanes), 1) < 2 * nf
    for c in range(gp_ref.shape[1] // t):
        rows = slice(c * t, (c + 1) * t)
        x = gp_ref[0, rows, :]
        lf = jax.nn.log_sigmoid(x)
        ipre = pltpu.roll(x, nf, 1)
        cum_f = jnp.dot(lower, lf, preferred_element_type=F32, precision=HIGHEST)
        cum_b = jnp.dot(upper, lf, preferred_element_type=F32, precision=HIGHEST)
        cum = jnp.where(fwd, cum_f, cum_b)
        tot = jnp.where(fwd_row, cum_f[t - 1:t, :], cum_b[0:1, :])
        gg = tot - cum + ipre
        cb_ref[0, rows, :] = cum
        gg_ref[0, rows, :] = gg
        at_ref[0, c] = (ipre - cum).T
        gt_ref[0, c] = gg.T


def _ml_gates(gp, nheads):
    b, l, lanes = gp.shape
    t = ML_CHUNK
    nc = l // t
    return pl.pallas_call(
        functools.partial(_ml_gates_kernel, nf=nheads, t=t),
        grid=(b,),
        in_specs=[pl.BlockSpec((1, l, lanes), lambda i: (i, 0, 0))],
        out_specs=[pl.BlockSpec((1, l, lanes), lambda i: (i, 0, 0)),
                   pl.BlockSpec((1, l, lanes), lambda i: (i, 0, 0)),
                   pl.BlockSpec((1, nc, lanes, t), lambda i: (i, 0, 0, 0)),
                   pl.BlockSpec((1, nc, lanes, t), lambda i: (i, 0, 0, 0))],
        out_shape=[_sds((b, l, lanes), F32), _sds((b, l, lanes), F32), _sds((b, nc, lanes, t), F32),
                   _sds((b, nc, lanes, t), F32)],
        compiler_params=_params("arbitrary"),
        name="ml_gates",
    )(gp)


def _ml_main_kernel(q_ref, kt_ref, v_ref, cb_ref, gg_ref, at_ref, gt_ref, xc_ref, z_ref, x_ref, gate_ref,
                    gnw_ref, skip_ref, wout_ref, *rest, nc, nh, final):
    if final:
        fg_ref, o_ref, c_scr, n_scr, m_scr, hf_scr = rest
    else:
        o_ref, c_scr, n_scr, m_scr, hf_scr = rest
    dirn = pl.program_id(1)
    step = pl.program_id(2)
    t = q_ref.shape[1]
    dh = q_ref.shape[2] // nh
    scale = float(dh) ** -0.5

    @pl.when(step == 0)
    def _():
        c_scr[...] = jnp.zeros_like(c_scr)
        n_scr[...] = jnp.zeros_like(n_scr)
        m_scr[...] = jnp.zeros_like(m_scr)

    ri = lax.broadcasted_iota(jnp.int32, (t, t), 0)
    ci = lax.broadcasted_iota(jnp.int32, (t, t), 1)
    ones_rows = jnp.ones((V7X_SUBLANES, t), BF16)

    def heads_out(d):
        keep = (ci <= ri) if d == 0 else (ci >= ri)
        stage = []
        for h in range(nh):
            cols = slice(h * dh, (h + 1) * dh)
            gcol = d * 2 * nh + nh + h
            q = q_ref[0, :, cols]
            kt = kt_ref[0, cols, :]
            cb = cb_ref[0, :, gcol:gcol + 1]
            gg = gg_ref[0, :, gcol:gcol + 1]
            a_row = at_ref[0, 0, gcol:gcol + 1, :]
            g_row = gt_ref[0, 0, gcol:gcol + 1, :]
            tot = cb[t - 1:t, :] if d == 0 else cb[0:1, :]
            m = m_scr[h][:, 0:1]
            dmat = jnp.where(keep, cb + a_row, -jnp.inf)
            inter = cb + m
            m_t = jnp.maximum(inter, jnp.max(dmat, axis=1, keepdims=True))
            m_new = jnp.maximum(tot + m, jnp.max(gg, axis=0, keepdims=True))
            stage.append(dict(
                q=q, kt=kt, m_t=m_t, m_new=m_new,
                w_intra=jnp.exp(dmat - m_t),
                w_inter=jnp.exp(inter - m_t),
                qk=jnp.dot(q, kt, preferred_element_type=F32),
                decay=jnp.exp(tot + m - m_new),
                wk_row=jnp.exp(g_row - m_new) * scale))
        for h in range(nh):
            cols = slice(h * dh, (h + 1) * dh)
            a = stage[h]
            v = v_ref[0, :, cols]
            s = a['qk'] * scale * a['w_intra']
            qw = a['q'] * a['w_inter'].astype(BF16)
            num = jnp.dot(s.astype(BF16), v, preferred_element_type=F32)
            a['num'] = num + jnp.dot(qw, c_scr[h].astype(BF16), preferred_element_type=F32)
            qn = jnp.sum(a['q'].astype(F32) * n_scr[h], axis=1, keepdims=True)
            a['nq'] = jnp.sum(s, axis=1, keepdims=True) + a['w_inter'] * qn
        for h in range(nh):
            cols = slice(h * dh, (h + 1) * dh)
            a = stage[h]
            v = v_ref[0, :, cols]
            a['hout'] = a['num'] / jnp.maximum(jnp.abs(a['nq']), jnp.exp(-a['m_t']))
            kwt = a['kt'] * a['wk_row'].astype(BF16)
            a['c_new'] = a['decay'] * c_scr[h] + jnp.dot(kwt, v, preferred_element_type=F32)
            a['n_add'] = _nt_dot(ones_rows, kwt)[0:1, :]
        for h in range(nh):
            a = stage[h]
            c_scr[h] = a['c_new']
            n_scr[h] = a['decay'] * n_scr[h] + a['n_add']
            m_scr[h] = jnp.broadcast_to(a['m_new'], m_scr.shape[1:])
        return [a['hout'] for a in stage]

    @pl.when(dirn == 0)
    def _():
        rows = pl.ds(pl.multiple_of(step * t, t), t)
        for h, hout in enumerate(heads_out(0)):
            hf_scr[rows, h * dh:(h + 1) * dh] = hout

    @pl.when(dirn == 1)
    def _():
        rows = pl.ds(pl.multiple_of((nc - 1 - step) * t, t), t)
        acc = None
        for h, hout in enumerate(heads_out(1)):
            cols = slice(h * dh, (h + 1) * dh)
            hs = hf_scr[rows, cols] + hout
            mu = jnp.mean(hs, axis=1, keepdims=True)
            cen = hs - mu
            var = jnp.mean(cen * cen, axis=1, keepdims=True)
            hn = cen * lax.rsqrt(var + LN_EPS)
            y = hn * gnw_ref[:, cols] + skip_ref[:, cols] * xc_ref[0, :, cols].astype(F32)
            y = y * _silu(z_ref[0, :, cols].astype(F32))
            part = jnp.dot(y.astype(BF16), wout_ref[cols, :], preferred_element_type=F32)
            acc = part if acc is None else acc + part
        out = x_ref[0] + gate_ref[0] * acc
        o_ref[0] = _rms(out, fg_ref[...]) if final else out


def _ml_main(q, kt, v, cb, gg, at, gt, xc, z, x, gate, gn_w, skip, w_out, nheads, final_g):
    b, l, e = q.shape
    d = x.shape[-1]
    t = ML_CHUNK
    nc = l // t
    lanes = cb.shape[-1]
    dh = e // nheads
    chunk = lambda i, dd, s: (i, s + dd * (nc - 1 - 2 * s), 0)
    chunk_t = lambda i, dd, s: (i, 0, s + dd * (nc - 1 - 2 * s))
    chunk4 = lambda i, dd, s: (i, s + dd * (nc - 1 - 2 * s), 0, 0)
    late = lambda i, dd, s: (i, (nc - 1) - dd * s, 0)
    vec = lambda i, dd, s: (0, 0)
    final = final_g is not None
    extra_specs = [pl.BlockSpec((1, d), vec)] if final else []
    extra_args = [final_g.reshape(1, d)] if final else []
    return pl.pallas_call(
        functools.partial(_ml_main_kernel, nc=nc, nh=nheads, final=final),
        grid=(b, 2, nc),
        in_specs=[pl.BlockSpec((1, t, e), chunk),
                  pl.BlockSpec((1, e, t), chunk_t),
                  pl.BlockSpec((1, t, e), chunk),
                  pl.BlockSpec((1, t, lanes), chunk),
                  pl.BlockSpec((1, t, lanes), chunk),
                  pl.BlockSpec((1, 1, lanes, t), chunk4),
                  pl.BlockSpec((1, 1, lanes, t), chunk4),
                  pl.BlockSpec((1, t, e), late),
                  pl.BlockSpec((1, t, e), late),
                  pl.BlockSpec((1, t, d), late),
                  pl.BlockSpec((1, 1, d), lambda i, dd, s: (i, 0, 0)),
                  pl.BlockSpec((1, e), vec),
                  pl.BlockSpec((1, e), vec),
                  pl.BlockSpec((e, d), vec)] + extra_specs,
        out_specs=pl.BlockSpec((1, t, d), late),
        out_shape=_sds((b, l, d), F32),
        scratch_shapes=[pltpu.VMEM((nheads, dh, dh), F32),
                        pltpu.VMEM((nheads, 1, dh), F32),
                        pltpu.VMEM((nheads, 1, V7X_LANES), F32),
                        pltpu.VMEM((l, e), F32)],
        compiler_params=_params("arbitrary", "arbitrary", "arbitrary"),
        name="ml_main",
    )(q, kt, v, cb, gg, at, gt, xc, z, x, gate, gn_w.reshape(1, e), skip.reshape(1, e), w_out.astype(BF16),
      *extra_args)


def _ml_layer(x, shift, scale, gate, g, w_in, conv_w, conv_b, w_q, w_k, w_v, w_gates, b_gates,
              gn_w, skip, w_out, final_g):
    assert x.shape[1] % ML_CHUNK == 0
    nheads = b_gates.shape[0] // 4
    xm, z = _ml_pre(x, shift, scale, g, w_in.astype(BF16))
    q, kt, v, xc, gp = _ml_qkv(xm, conv_w, conv_b, w_q, w_k, w_v, w_gates, b_gates)
    cb, gg, at, gt = _ml_gates(gp, nheads)
    return _ml_main(q, kt, v, cb, gg, at, gt, xc, z, x, gate, gn_w, skip, w_out, nheads, final_g)


def _final_norm_kernel(x_ref, g_ref, o_ref):
    o_ref[0] = _rms(x_ref[0], g_ref[...])


def _final_norm(x, g):
    b, l, d = x.shape
    tl = min(TOKEN_TILE, l)
    return pl.pallas_call(
        _final_norm_kernel,
        grid=(b, l // tl),
        in_specs=[pl.BlockSpec((1, tl, d), lambda i, j: (i, j, 0)),
                  pl.BlockSpec((1, d), lambda i, j: (0, 0))],
        out_specs=pl.BlockSpec((1, tl, d), lambda i, j: (i, j, 0)),
        out_shape=_sds((b, l, d), F32),
        compiler_params=_params("arbitrary", "arbitrary"),
        name="final_norm",
    )(x, g.reshape(1, d))


def kernel(x, c, ada_w, ada_b, norm_g, s5_w_in, s5_lam_re, s5_lam_im, s5_log_dt, s5_b_re, s5_b_im, s5_c_re, s5_c_im, s5_d, s5_w_glu, s5_b_glu, s5_w_out, ml_w_in, ml_conv_w, ml_conv_b, ml_w_q, ml_w_k, ml_w_v, ml_w_gates, ml_b_gates, ml_gn_w, ml_skip, ml_w_out, final_g):
    depth = ada_w.shape[0]
    d = x.shape[-1]
    mod = _ada(c, ada_w, ada_b)
    for i in range(depth):
        shift = mod[i, :, None, 0:d]
        scale = mod[i, :, None, d:2 * d]
        gate = mod[i, :, None, 2 * d:3 * d]
        g = norm_g[i].reshape(1, d)
        j = i // 2
        if i % 2 == 0:
            x = _s5_layer(x, shift, scale, gate, g, s5_w_in[j], s5_lam_re[j], s5_lam_im[j], s5_log_dt[j],
                          s5_b_re[j], s5_b_im[j], s5_c_re[j], s5_c_im[j], s5_d[j], s5_w_glu[j],
                          s5_b_glu[j], s5_w_out[j])
        else:
            x = _ml_layer(x, shift, scale, gate, g, ml_w_in[j], ml_conv_w[j], ml_conv_b[j], ml_w_q[j],
                          ml_w_k[j], ml_w_v[j], ml_w_gates[j], ml_b_gates[j], ml_gn_w[j], ml_skip[j],
                          ml_w_out[j], final_g if i == depth - 1 else None)
    return x if depth % 2 == 0 else _final_norm(x, final_g)
```

```python
import functools

import jax
import jax.numpy as jnp
from jax import lax
from jax.experimental import pallas as pl
from jax.experimental.pallas import tpu as pltpu

F32 = jnp.float32
BF16 = jnp.bfloat16
U32 = jnp.uint32
HIGHEST = lax.Precision.HIGHEST

RMS_EPS = 1e-6
LN_EPS = 1e-5
V7X_LANES = 128
V7X_SUBLANES = 8
V7X_MXU_DIM = 256
VMEM_LIMIT_BYTES = 56 * 1024 * 1024
S5_CHUNK = V7X_LANES
S5_TILE = V7X_SUBLANES * S5_CHUNK
S5_SUB = 512
ML_CHUNK = V7X_MXU_DIM
TOKEN_TILE = 512


def _params(*sem):
    return pltpu.CompilerParams(dimension_semantics=sem, vmem_limit_bytes=VMEM_LIMIT_BYTES)


def _sds(shape, dtype):
    return jax.ShapeDtypeStruct(shape, dtype)


def _sigmoid(v):
    return 0.5 * jnp.tanh(0.5 * v) + 0.5


def _silu(v):
    return v * _sigmoid(v)


def _nt_dot(a, b, precision=None):
    return lax.dot_general(a, b, (((1,), (1,)), ((), ())), preferred_element_type=F32,
                           precision=precision)


def _cat(parts, axis):
    return parts[0] if len(parts) == 1 else jnp.concatenate(parts, axis=axis)


def _rms(x, g):
    ms = jnp.mean(x * x, axis=-1, keepdims=True)
    return x * lax.rsqrt(ms + RMS_EPS) * g


def _modulated_norm(x, g, shift, scale):
    return _rms(x, g) * (1.0 + scale) + shift


def _ada_kernel(c_ref, w_ref, b_ref, o_ref):
    sc = _silu(c_ref[...])
    o_ref[0] = jnp.dot(sc, w_ref[0], preferred_element_type=F32, precision=HIGHEST) + b_ref[0]


def _ada(c, ada_w, ada_b):
    depth, d, d3 = ada_w.shape
    b = c.shape[0]
    return pl.pallas_call(
        _ada_kernel,
        grid=(depth, d3 // d),
        in_specs=[pl.BlockSpec((b, d), lambda i, j: (0, 0)),
                  pl.BlockSpec((1, d, d), lambda i, j: (i, 0, j)),
                  pl.BlockSpec((1, 1, d), lambda i, j: (i, 0, j))],
        out_specs=pl.BlockSpec((1, b, d), lambda i, j: (i, 0, j)),
        out_shape=_sds((depth, b, d3), F32),
        compiler_params=_params("arbitrary", "arbitrary"),
        name="ada_mod",
    )(c, ada_w, ada_b.reshape(depth, 1, d3))


def _relayout_pitch(e):
    pitch = e + V7X_SUBLANES
    assert pitch % (2 * V7X_SUBLANES) == V7X_SUBLANES
    return pitch


def _s5_pre_kernel(x_ref, shift_ref, scale_ref, g_ref, wt_ref, ut_ref, uzt_ref, rl_scr, *, e, t, sub, pitch):
    kk = sub // t
    for s in range(x_ref.shape[1] // sub):
        h = _modulated_norm(x_ref[0, s * sub:(s + 1) * sub, :], g_ref[...], shift_ref[0], scale_ref[0])
        uz = _nt_dot(wt_ref[...], h.astype(BF16))
        uzt_ref[0, :, s * sub:(s + 1) * sub] = uz.astype(BF16)
        for k in range(kk):
            c = s * kk + k
            rl_scr[c * pitch:c * pitch + e, :] = uz[:e, k * t:(k + 1) * t]

    for ch in range(e):
        ut_ref[ch] = rl_scr[pl.ds(ch, V7X_SUBLANES, stride=pitch), :]


def _s5_pre(x, shift, scale, g, w_in_t):
    b, l, d = x.shape
    e = w_in_t.shape[0] // 2
    t, tl = S5_CHUNK, S5_TILE
    steps = l // tl
    k = tl // t
    pitch = _relayout_pitch(e)
    return pl.pallas_call(
        functools.partial(_s5_pre_kernel, e=e, t=t, sub=min(S5_SUB, tl), pitch=pitch),
        grid=(b, steps),
        in_specs=[pl.BlockSpec((1, tl, d), lambda i, j: (i, j, 0)),
                  pl.BlockSpec((1, 1, d), lambda i, j: (i, 0, 0)),
                  pl.BlockSpec((1, 1, d), lambda i, j: (i, 0, 0)),
                  pl.BlockSpec((1, d), lambda i, j: (0, 0)),
                  pl.BlockSpec((2 * e, d), lambda i, j: (0, 0))],
        out_specs=[pl.BlockSpec((e, k, t), lambda i, j: (0, i * steps + j, 0)),
                   pl.BlockSpec((1, 2 * e, tl), lambda i, j: (i, 0, j))],
        out_shape=[_sds((e, b * l // t, t), F32), _sds((b, 2 * e, l), BF16)],
        scratch_shapes=[pltpu.VMEM((k * pitch, t), F32)],
        compiler_params=_params("arbitrary", "arbitrary"),
        name="s5_pre",
    )(x, shift, scale, g, w_in_t)


def _abar(lre, lim, ldt):
    dt = jnp.exp(ldt)
    mag = jnp.exp(lre * dt)
    are = mag * jnp.cos(lim * dt)
    aim = mag * jnp.sin(lim * dt)
    den = lre * lre + lim * lim
    zre = ((are - 1.0) * lre + aim * lim) / den
    zim = (aim * lre - (are - 1.0) * lim) / den
    return are, aim, zre, zim


def _cplx_pow(ar, ai, expo, nbits):
    shape = expo.shape
    rr = jnp.ones(shape, F32)
    ri = jnp.zeros(shape, F32)
    pr, pi = ar, ai
    for k in range(nbits):
        bit = ((expo >> k) & 1) == 1
        mr = jnp.where(bit, pr, 1.0)
        mi = jnp.where(bit, pi, 0.0)
        rr, ri = rr * mr - ri * mi, rr * mi + ri * mr
        if k + 1 < nbits:
            pr, pi = pr * pr - pi * pi, 2.0 * pr * pi
    return rr, ri


def _s5_tables(prm, xmat_scr, ymat_scr, adec_scr, kw_scr, *, t, n, p):
    lre, lim, ldt, btre, btim, cre, cim = prm
    n2 = 2 * n
    nbits = (t - 1).bit_length()
    lane_row = lax.broadcasted_iota(jnp.int32, (1, n2), 1)
    lane_tn = lax.broadcasted_iota(jnp.int32, (t, n2), 1)
    sub_tn = lax.broadcasted_iota(jnp.int32, (t, n2), 0)
    lane_p = lax.broadcasted_iota(jnp.int32, (p, n2), 1)
    lane_pp = lax.broadcasted_iota(jnp.int32, (p * p, n2), 1)
    first_row = lane_row < n
    first_tn = lane_tn < n

    k_halves = []
    k0_bwd = None
    for d in range(2):
        are, aim, zre, zim = _abar(lre[d], lim[d], ldt[d])

        pr, pi = are, aim
        for _ in range(t.bit_length() - 1):
            pr, pi = pr * pr - pi * pi, 2.0 * pr * pi
        adec_scr[2 * d:2 * d + 1, :] = pr
        adec_scr[2 * d + 1:2 * d + 2, :] = jnp.where(first_row, -pi, pi)

        bbt_re = zre * btre[d] - zim * btim[d]
        bbt_im = zre * btim[d] + zim * btre[d]

        asc_re, asc_im = _cplx_pow(are, aim, sub_tn, nbits)
        dsc_re, dsc_im = _cplx_pow(are, aim, t - 1 - sub_tn, nbits)
        up_re, up_im = (asc_re, asc_im) if d == 0 else (dsc_re, dsc_im)
        up_re, up_im = up_re * are - up_im * aim, up_re * aim + up_im * are

        rows = []
        for pp in range(p):
            c_re = cre[d][pp:pp + 1, :]
            c_im = cim[d][pp:pp + 1, :]
            cb_re = c_re * bbt_re - c_im * bbt_im
            cb_im = c_re * bbt_im + c_im * bbt_re
            rows.append(jnp.where(lane_p < n, cb_re, -cb_im))
        cbw = jnp.concatenate(rows, axis=0)
        lag_re, lag_im = (asc_re, asc_im) if d == 0 else (up_re, up_im)
        k_halves.append(_nt_dot(cbw, jnp.where(first_tn, lag_re, lag_im), precision=HIGHEST))
        if d == 1:
            k0_bwd = jnp.sum(jnp.where(lane_pp < n, cbw, 0.0), axis=1, keepdims=True)

        xt_re, xt_im = (dsc_re, dsc_im) if d == 0 else (asc_re, asc_im)
        for q in range(p):
            b_re = bbt_re[q:q + 1, :]
            b_im = bbt_im[q:q + 1, :]
            v1 = jnp.where(first_row, b_re, b_im)
            v2 = jnp.where(first_row, b_im, -b_re)
            xmat_scr[q * t:(q + 1) * t, d * n2:(d + 1) * n2] = (xt_re * v1 - xt_im * v2).astype(BF16)

        ya = jnp.where(first_tn, up_re, -up_im)
        yb = jnp.where(first_tn, up_im, up_re)
        for pp in range(p):
            ymat_scr[pp * t:(pp + 1) * t, d * n2:(d + 1) * n2] = (
                cre[d][pp:pp + 1, :] * ya - cim[d][pp:pp + 1, :] * yb).astype(BF16)

    lane_k = lax.broadcasted_iota(jnp.int32, (p * p, t), 1)
    kfull = jnp.concatenate([k_halves[0] + jnp.where(lane_k == 0, k0_bwd, 0.0), k_halves[1]], axis=1)
    kprev = pltpu.roll(kfull, 1, 1)
    lo = pltpu.bitcast(kfull.astype(BF16).astype(F32), U32) >> 16
    hi = pltpu.bitcast(kprev.astype(BF16).astype(F32), U32) & jnp.uint32(0xFFFF0000)
    words = hi | lo
    kw_scr[0] = words[:, :t]
    kw_scr[1] = words[:, t:]


def _s5_wrap_mask(t):
    wr = lax.broadcasted_iota(jnp.int32, (t // 2, t), 0)
    ln = lax.broadcasted_iota(jnp.int32, (t // 2, t), 1)
    return ln + 2 * wr <= t - 1


def _s5_toep_tile(kw_scr, toep_scr, q, pp, wrap_mask, *, t, p):
    r = pp * p + q
    full = jnp.where(wrap_mask, kw_scr[0, pl.ds(r, 1), :], kw_scr[1, pl.ds(r, 1), :])
    rolled = pltpu.roll(full, 0, 1, stride=2, stride_axis=0)
    start = q * t if isinstance(q, int) else pl.multiple_of(q * t, t)
    toep_scr[pl.ds(start, t), pp * t:(pp + 1) * t] = pltpu.bitcast(rolled, BF16)


def _s5_ssm_kernel(u_ref, *refs, nc, p, t, n):
    cur_prm, nxt_prm, y_ref = refs[0:7], refs[7:14], refs[14]
    toep_scr, xmat_scr, ymat_scr, adec_scr, kw_scr, lhs_scr, xs_scr, xw_scr, sin_scr = refs[15:]
    n2 = 2 * n
    bsz = u_ref.shape[1] // nc
    g = pl.program_id(0)
    slot = lax.rem(g, 2)
    nslot = 1 - slot
    kw = dict(t=t, n=n, p=p)
    wrap_mask = _s5_wrap_mask(t)

    @pl.when(g == 0)
    def _():
        _s5_tables(cur_prm, xmat_scr.at[0], ymat_scr.at[0], adec_scr.at[0], kw_scr.at[0], **kw)

        def q_body(q, carry):
            for pp in range(p):
                _s5_toep_tile(kw_scr.at[0], toep_scr.at[0], q, pp, wrap_mask, t=t, p=p)
            return carry

        lax.fori_loop(0, p, q_body, 0)

    _s5_tables(nxt_prm, xmat_scr.at[nslot], ymat_scr.at[nslot], adec_scr.at[nslot], kw_scr.at[nslot], **kw)

    for q in range(p):
        lhs_scr[:, q * t:(q + 1) * t] = u_ref[q].astype(BF16)
    lhs = lhs_scr[...]

    xmat = xmat_scr.at[slot]
    ymat = ymat_scr.at[slot]
    adec = adec_scr.at[slot]
    toep = toep_scr.at[slot]
    xs_both = jnp.dot(lhs, xmat[...], preferred_element_type=F32)
    for d in range(2):
        xs = xs_both[:, d * n2:(d + 1) * n2]
        xs_scr[d] = xs
        xw_scr[d] = pltpu.roll(xs, n, 1)

    for d in range(2):
        a1 = adec[2 * d:2 * d + 1, :]
        a2 = adec[2 * d + 1:2 * d + 2, :]
        s = jnp.zeros((bsz, n2), F32)
        sw = jnp.zeros((bsz, n2), F32)
        for c in (range(nc) if d == 0 else reversed(range(nc))):
            rows = pl.ds(c, bsz, stride=nc)
            sin_scr[d, rows, :] = s
            s, sw = (a1 * s + a2 * sw + xs_scr[d, rows, :],
                     a1 * sw - a2 * s + xw_scr[d, rows, :])

    sin_both = jnp.concatenate([sin_scr[0].astype(BF16), sin_scr[1].astype(BF16)], axis=1)
    w = V7X_MXU_DIM
    npanels = p * t // w
    tiles = [(q, pp) for q in range(p) for pp in range(p)]
    per_panel = len(tiles) // npanels
    for pn in range(npanels):
        cols = slice(pn * w, (pn + 1) * w)
        acc = jnp.dot(lhs, toep[:, cols], preferred_element_type=F32)
        acc = acc + _nt_dot(sin_both, ymat[cols, :])
        for h in range(w // t):
            y_ref[pn * (w // t) + h] = acc[:, h * t:(h + 1) * t]
        for q, pp in tiles[pn * per_panel:(pn + 1) * per_panel]:
            _s5_toep_tile(kw_scr.at[nslot], toep_scr.at[nslot], q, pp, wrap_mask, t=t, p=p)


def _s5_ssm(ut, nc, lam_re, lam_im, log_dt, b_re, b_im, c_re, c_im):
    e, bc, t = ut.shape
    _, g, n, p = b_re.shape
    n2 = 2 * n
    pt = p * t
    dup = lambda a: jnp.concatenate([a, a], axis=-1)
    lre = dup(lam_re)[:, :, None, :]
    lim = dup(lam_im)[:, :, None, :]
    ldt = log_dt[:, :, None, None]
    btre = dup(jnp.swapaxes(b_re, -1, -2))
    btim = dup(jnp.swapaxes(b_im, -1, -2))
    cre = dup(c_re)
    cim = dup(c_im)
    prm = (lre, lim, ldt, btre, btim, cre, cim)

    def this_group(a):
        return pl.BlockSpec((2, None) + a.shape[2:], lambda i: (0, i, 0, 0))

    def next_group(a):
        return pl.BlockSpec((2, None) + a.shape[2:], lambda i: (0, jnp.minimum(i + 1, g - 1), 0, 0))

    return pl.pallas_call(
        functools.partial(_s5_ssm_kernel, nc=nc, p=p, t=t, n=n),
        grid=(g,),
        in_specs=([pl.BlockSpec((p, bc, t), lambda i: (i, 0, 0))]
                  + [this_group(a) for a in prm] + [next_group(a) for a in prm]),
        out_specs=pl.BlockSpec((p, bc, t), lambda i: (i, 0, 0)),
        out_shape=_sds((e, bc, t), F32),
        scratch_shapes=[pltpu.VMEM((2, pt, pt), BF16),
                        pltpu.VMEM((2, pt, 2 * n2), BF16),
                        pltpu.VMEM((2, pt, 2 * n2), BF16),
                        pltpu.VMEM((2, 4, n2), F32),
                        pltpu.VMEM((2, 2, p * p, t), U32),
                        pltpu.VMEM((bc, pt), BF16),
                        pltpu.VMEM((2, bc, n2), F32),
                        pltpu.VMEM((2, bc, n2), F32),
                        pltpu.VMEM((2, bc, n2), F32)],
        compiler_params=_params("arbitrary"),
        name="s5_ssm",
    )(ut, *prm, *prm)


def _s5_post_kernel(y_ref, uzt_ref, x_ref, gate_ref, dcol_ref, bcol_ref, wgt_ref, wot_ref,
                    o_ref, rl_scr, *, e, t, sub, pitch):
    for ch in range(e):
        rl_scr[pl.ds(ch, V7X_SUBLANES, stride=pitch), :] = y_ref[ch]

    k = sub // t
    reps = sub // dcol_ref.shape[1]
    dfull = _cat([dcol_ref[...]] * reps, 1)
    bfull = _cat([bcol_ref[...]] * reps, 1)
    for s in range(x_ref.shape[1] // sub):
        tok = slice(s * sub, (s + 1) * sub)
        yt = _cat([rl_scr[(s * k + kk) * pitch:(s * k + kk) * pitch + e, :] for kk in range(k)], 1)
        ut = uzt_ref[0, :e, tok].astype(F32)
        y = jax.nn.gelu(yt + dfull * ut)
        g = jnp.dot(wgt_ref[...], y.astype(BF16), preferred_element_type=F32) + bfull
        y = y * _sigmoid(g)
        z = uzt_ref[0, e:, tok].astype(F32)
        yz = y * _silu(z)
        ot = jnp.dot(wot_ref[...], yz.astype(BF16), preferred_element_type=F32)
        o_ref[0, tok, :] = x_ref[0, tok, :] + gate_ref[0] * ot.T


def _s5_post(yt, uzt, x, gate, d_skip, b_glu, w_glu_t, w_out_t):
    b, l, d = x.shape
    e, bc, t = yt.shape
    tl = S5_TILE
    k = tl // t
    lanes = V7X_LANES
    dcol = jnp.broadcast_to(d_skip[:, None], (e, lanes))
    bcol = jnp.broadcast_to(b_glu[:, None], (e, lanes))
    steps = l // tl
    pitch = _relayout_pitch(e)
    return pl.pallas_call(
        functools.partial(_s5_post_kernel, e=e, t=t, sub=min(S5_SUB, tl), pitch=pitch),
        grid=(b, steps),
        in_specs=[pl.BlockSpec((e, k, t), lambda i, j: (0, i * steps + j, 0)),
                  pl.BlockSpec((1, 2 * e, tl), lambda i, j: (i, 0, j)),
                  pl.BlockSpec((1, tl, d), lambda i, j: (i, j, 0)),
                  pl.BlockSpec((1, 1, d), lambda i, j: (i, 0, 0)),
                  pl.BlockSpec((e, lanes), lambda i, j: (0, 0)),
                  pl.BlockSpec((e, lanes), lambda i, j: (0, 0)),
                  pl.BlockSpec((e, e), lambda i, j: (0, 0)),
                  pl.BlockSpec((d, e), lambda i, j: (0, 0))],
        out_specs=pl.BlockSpec((1, tl, d), lambda i, j: (i, j, 0)),
        out_shape=_sds((b, l, d), F32),
        scratch_shapes=[pltpu.VMEM((k * pitch, t), F32)],
        compiler_params=_params("arbitrary", "arbitrary"),
        name="s5_post",
    )(yt, uzt, x, gate, dcol, bcol, w_glu_t, w_out_t)


def _s5_layer(x, shift, scale, gate, g, w_in, lam_re, lam_im, log_dt, b_re, b_im, c_re, c_im,
              d_skip, w_glu, b_glu, w_out):
    l = x.shape[1]
    assert l % S5_TILE == 0
    ut, uzt = _s5_pre(x, shift, scale, g, w_in.T.astype(BF16))
    yt = _s5_ssm(ut, l // S5_CHUNK, lam_re, lam_im, log_dt, b_re, b_im, c_re, c_im)
    return _s5_post(yt, uzt, x, gate, d_skip, b_glu, w_glu.T.astype(BF16), w_out.T.astype(BF16))


def _ml_pre_kernel(x_ref, shift_ref, scale_ref, g_ref, w_ref, xm_ref, z_ref, *, e):
    h = _modulated_norm(x_ref[0], g_ref[...], shift_ref[0], scale_ref[0])
    xz = jnp.dot(h.astype(BF16), w_ref[...], preferred_element_type=F32)
    xm_ref[0] = xz[:, :e].astype(BF16)
    z_ref[0] = xz[:, e:].astype(BF16)


def _ml_pre(x, shift, scale, g, w_in):
    b, l, d = x.shape
    e = w_in.shape[1] // 2
    tl = min(TOKEN_TILE, l)
    return pl.pallas_call(
        functools.partial(_ml_pre_kernel, e=e),
        grid=(b, l // tl),
        in_specs=[pl.BlockSpec((1, tl, d), lambda i, j: (i, j, 0)),
                  pl.BlockSpec((1, 1, d), lambda i, j: (i, 0, 0)),
                  pl.BlockSpec((1, 1, d), lambda i, j: (i, 0, 0)),
                  pl.BlockSpec((1, d), lambda i, j: (0, 0)),
                  pl.BlockSpec((d, 2 * e), lambda i, j: (0, 0))],
        out_specs=[pl.BlockSpec((1, tl, e), lambda i, j: (i, j, 0)),
                   pl.BlockSpec((1, tl, e), lambda i, j: (i, j, 0))],
        out_shape=[_sds((b, l, e), BF16), _sds((b, l, e), BF16)],
        compiler_params=_params("arbitrary", "arbitrary"),
        name="ml_pre",
    )(x, shift, scale, g, w_in)


HALO = 2 * V7X_SUBLANES


def _ml_qkv_kernel(xm_ref, prev_ref, next_ref, cw_ref, cb_ref, wq_ref, wk_ref, wv_ref, wg_ref,
                   bg_ref, q_ref, kt_ref, v_ref, xc_ref, gp_ref, *, steps, taps):
    j = pl.program_id(1)
    xm_b = xm_ref[0]
    tq, e = xm_b.shape
    zero = jnp.zeros((), BF16)
    prev = jnp.where(j == 0, zero, prev_ref[0])
    nxt = jnp.where(j == steps - 1, zero, next_ref[0])
    xe = jnp.concatenate([prev, xm_b, nxt], axis=0).astype(F32)
    pad = taps // 2
    acc = jnp.broadcast_to(cb_ref[...], (tq, e))
    for kk in range(taps):
        off = HALO + kk - pad
        acc = acc + xe[off:off + tq, :] * cw_ref[kk:kk + 1, :]
    xc = _silu(acc)
    xc_b = xc.astype(BF16)
    tile = wq_ref.shape[-1]
    qs, ks, vs = [], [], []
    for i in range(e // tile):
        cols = slice(i * tile, (i + 1) * tile)
        qs.append(jnp.dot(xc_b[:, cols], wq_ref[i], preferred_element_type=F32))
        ks.append(jnp.dot(xc_b[:, cols], wk_ref[i], preferred_element_type=F32).astype(BF16))
        vs.append(jnp.dot(xm_b[:, cols], wv_ref[i], preferred_element_type=F32))
    q = jnp.concatenate(qs, axis=1).astype(BF16)
    k = jnp.concatenate(ks, axis=1)
    v = jnp.concatenate(vs, axis=1).astype(BF16)
    gp = (jnp.dot(q, wg_ref[0], preferred_element_type=F32)
          + jnp.dot(k, wg_ref[1], preferred_element_type=F32)
          + jnp.dot(v, wg_ref[2], preferred_element_type=F32) + bg_ref[...])
    q_ref[0] = q
    for i in range(e // tile):
        kt_ref[0, i * tile:(i + 1) * tile, :] = ks[i].T
    v_ref[0] = v
    xc_ref[0] = xc_b
    gp_ref[0] = gp


def _blockdiag_tiles(w, tile):
    nblk, bs, _ = w.shape
    rows = w.reshape(nblk // (tile // bs), tile, bs)
    spread = jnp.tile(rows, (1, 1, tile // bs))
    blk = jnp.arange(tile) // bs
    return jnp.where(blk[:, None] == blk[None, :], spread, 0.0)


def _ml_qkv(xm, conv_w, conv_b, w_q, w_k, w_v, w_gates, b_gates):
    b, l, e = xm.shape
    tq = min(TOKEN_TILE, l)
    steps = l // tq
    taps = conv_w.shape[0]
    tile = V7X_MXU_DIM
    lanes = V7X_LANES
    ng = w_gates.shape[-1]
    wq = _blockdiag_tiles(w_q, tile).astype(BF16)
    wk = _blockdiag_tiles(w_k, tile).astype(BF16)
    wv = _blockdiag_tiles(w_v, tile).astype(BF16)
    wg = jnp.pad(w_gates, ((0, 0), (0, 0), (0, lanes - ng))).astype(BF16)
    bg = jnp.pad(b_gates, (0, lanes - ng)).reshape(1, lanes)
    hb = tq // HALO
    last = l // HALO - 1
    tok = lambda i, j: (i, j, 0)
    full3 = lambda i, j: (0, 0, 0)
    full2 = lambda i, j: (0, 0)
    return pl.pallas_call(
        functools.partial(_ml_qkv_kernel, steps=steps, taps=taps),
        grid=(b, steps),
        in_specs=[pl.BlockSpec((1, tq, e), tok),
                  pl.BlockSpec((1, HALO, e), lambda i, j: (i, jnp.maximum(j * hb - 1, 0), 0)),
                  pl.BlockSpec((1, HALO, e), lambda i, j: (i, jnp.minimum((j + 1) * hb, last), 0)),
                  pl.BlockSpec((taps, e), full2),
                  pl.BlockSpec((1, e), full2),
                  pl.BlockSpec(wq.shape, full3),
                  pl.BlockSpec(wk.shape, full3),
                  pl.BlockSpec(wv.shape, full3),
                  pl.BlockSpec(wg.shape, full3),
                  pl.BlockSpec((1, lanes), full2)],
        out_specs=[pl.BlockSpec((1, tq, e), tok), pl.BlockSpec((1, e, tq), lambda i, j: (i, 0, j)),
                   pl.BlockSpec((1, tq, e), tok), pl.BlockSpec((1, tq, e), tok),
                   pl.BlockSpec((1, tq, lanes), tok)],
        out_shape=[_sds((b, l, e), BF16), _sds((b, e, l), BF16), _sds((b, l, e), BF16),
                   _sds((b, l, e), BF16), _sds((b, l, lanes), F32)],
        compiler_params=_params("arbitrary", "arbitrary"),
        name="ml_qkv",
    )(xm, xm, xm, conv_w, conv_b.reshape(1, e), wq, wk, wv, wg, bg)


def _ml_gates_kernel(gp_ref, cb_ref, gg_ref, at_ref, gt_ref, *, nf, t):
    lanes = gp_ref.shape[2]
    ri = lax.broadcasted_iota(jnp.int32, (t, t), 0)
    ci = lax.broadcasted_iota(jnp.int32, (t, t), 1)
    lower = jnp.where(ci <= ri, 1.0, 0.0)
    upper = jnp.where(ci >= ri, 1.0, 0.0)
    fwd = lax.broadcasted_iota(jnp.int32, (t, lanes), 1) < 2 * nf
    fwd_row = lax.broadcasted_iota(jnp.int32, (1, lanes), 1) < 2 * nf
    for c in range(gp_ref.shape[1] // t):
        rows = slice(c * t, (c + 1) * t)
        x = gp_ref[0, rows, :]
        lf = jax.nn.log_sigmoid(x)
        ipre = pltpu.roll(x, nf, 1)
        cum_f = jnp.dot(lower, lf, preferred_element_type=F32, precision=HIGHEST)
        cum_b = jnp.dot(upper, lf, preferred_element_type=F32, precision=HIGHEST)
        cum = jnp.where(fwd, cum_f, cum_b)
        tot = jnp.where(fwd_row, cum_f[t - 1:t, :], cum_b[0:1, :])
        gg = tot - cum + ipre
        cb_ref[0, rows, :] = cum
        gg_ref[0, rows, :] = gg
        at_ref[0, c] = (ipre - cum).T
        gt_ref[0, c] = gg.T


def _ml_gates(gp, nheads):
    b, l, lanes = gp.shape
    t = ML_CHUNK
    nc = l // t
    return pl.pallas_call(
        functools.partial(_ml_gates_kernel, nf=nheads, t=t),
        grid=(b,),
        in_specs=[pl.BlockSpec((1, l, lanes), lambda i: (i, 0, 0))],
        out_specs=[pl.BlockSpec((1, l, lanes), lambda i: (i, 0, 0)),
                   pl.BlockSpec((1, l, lanes), lambda i: (i, 0, 0)),
                   pl.BlockSpec((1, nc, lanes, t), lambda i: (i, 0, 0, 0)),
                   pl.BlockSpec((1, nc, lanes, t), lambda i: (i, 0, 0, 0))],
        out_shape=[_sds((b, l, lanes), F32), _sds((b, l, lanes), F32), _sds((b, nc, lanes, t), F32),
                   _sds((b, nc, lanes, t), F32)],
        compiler_params=_params("arbitrary"),
        name="ml_gates",
    )(gp)


def _ml_main_kernel(q_ref, kt_ref, v_ref, cb_ref, gg_ref, at_ref, gt_ref, xc_ref, z_ref, x_ref, gate_ref,
                    gnw_ref, skip_ref, wout_ref, *rest, nc, nh, final):
    if final:
        fg_ref, o_ref, c_scr, n_scr, m_scr, hf_scr = rest
    else:
        o_ref, c_scr, n_scr, m_scr, hf_scr = rest
    dirn = pl.program_id(1)
    step = pl.program_id(2)
    t = q_ref.shape[1]
    dh = q_ref.shape[2] // nh
    scale = float(dh) ** -0.5

    @pl.when(step == 0)
    def _():
        c_scr[...] = jnp.zeros_like(c_scr)
        n_scr[...] = jnp.zeros_like(n_scr)
        m_scr[...] = jnp.zeros_like(m_scr)

    ri = lax.broadcasted_iota(jnp.int32, (t, t), 0)
    ci = lax.broadcasted_iota(jnp.int32, (t, t), 1)
    ones_rows = jnp.ones((V7X_SUBLANES, t), BF16)

    def heads_out(d):
        keep = (ci <= ri) if d == 0 else (ci >= ri)
        stage = []
        for h in range(nh):
            cols = slice(h * dh, (h + 1) * dh)
            gcol = d * 2 * nh + nh + h
            q = q_ref[0, :, cols]
            kt = kt_ref[0, cols, :]
            cb = cb_ref[0, :, gcol:gcol + 1]
            gg = gg_ref[0, :, gcol:gcol + 1]
            a_row = at_ref[0, 0, gcol:gcol + 1, :]
            g_row = gt_ref[0, 0, gcol:gcol + 1, :]
            tot = cb[t - 1:t, :] if d == 0 else cb[0:1, :]
            m = m_scr[h][:, 0:1]
            dmat = jnp.where(keep, cb + a_row, -jnp.inf)
            inter = cb + m
            m_t = jnp.maximum(inter, jnp.max(dmat, axis=1, keepdims=True))
            m_new = jnp.maximum(tot + m, jnp.max(gg, axis=0, keepdims=True))
            stage.append(dict(
                q=q, kt=kt, m_t=m_t, m_new=m_new,
                w_intra=jnp.exp(dmat - m_t),
                w_inter=jnp.exp(inter - m_t),
                qk=jnp.dot(q, kt, preferred_element_type=F32),
                decay=jnp.exp(tot + m - m_new),
                wk_row=jnp.exp(g_row - m_new) * scale))
        for h in range(nh):
            cols = slice(h * dh, (h + 1) * dh)
            a = stage[h]
            v = v_ref[0, :, cols]
            s = a['qk'] * scale * a['w_intra']
            qw = a['q'] * a['w_inter'].astype(BF16)
            num = jnp.dot(s.astype(BF16), v, preferred_element_type=F32)
            a['num'] = num + jnp.dot(qw, c_scr[h].astype(BF16), preferred_element_type=F32)
            qn = jnp.sum(a['q'].astype(F32) * n_scr[h], axis=1, keepdims=True)
            a['nq'] = jnp.sum(s, axis=1, keepdims=True) + a['w_inter'] * qn
        for h in range(nh):
            cols = slice(h * dh, (h + 1) * dh)
            a = stage[h]
            v = v_ref[0, :, cols]
            a['hout'] = a['num'] / jnp.maximum(jnp.abs(a['nq']), jnp.exp(-a['m_t']))
            kwt = a['kt'] * a['wk_row'].astype(BF16)
            a['c_new'] = a['decay'] * c_scr[h] + jnp.dot(kwt, v, preferred_element_type=F32)
            a['n_add'] = _nt_dot(ones_rows, kwt)[0:1, :]
        for h in range(nh):
            a = stage[h]
            c_scr[h] = a['c_new']
            n_scr[h] = a['decay'] * n_scr[h] + a['n_add']
            m_scr[h] = jnp.broadcast_to(a['m_new'], m_scr.shape[1:])
        return [a['hout'] for a in stage]

    @pl.when(dirn == 0)
    def _():
        rows = pl.ds(pl.multiple_of(step * t, t), t)
        for h, hout in enumerate(heads_out(0)):
            hf_scr[rows, h * dh:(h + 1) * dh] = hout

    @pl.when(dirn == 1)
    def _():
        rows = pl.ds(pl.multiple_of((nc - 1 - step) * t, t), t)
        acc = None
        for h, hout in enumerate(heads_out(1)):
            cols = slice(h * dh, (h + 1) * dh)
            hs = hf_scr[rows, cols] + hout
            mu = jnp.mean(hs, axis=1, keepdims=True)
            cen = hs - mu
            var = jnp.mean(cen * cen, axis=1, keepdims=True)
            hn = cen * lax.rsqrt(var + LN_EPS)
            y = hn * gnw_ref[:, cols] + skip_ref[:, cols] * xc_ref[0, :, cols].astype(F32)
            y = y * _silu(z_ref[0, :, cols].astype(F32))
            part = jnp.dot(y.astype(BF16), wout_ref[cols, :], preferred_element_type=F32)
            acc = part if acc is None else acc + part
        out = x_ref[0] + gate_ref[0] * acc
        o_ref[0] = _rms(out, fg_ref[...]) if final else out


def _ml_main(q, kt, v, cb, gg, at, gt, xc, z, x, gate, gn_w, skip, w_out, nheads, final_g):
    b, l, e = q.shape
    d = x.shape[-1]
    t = ML_CHUNK
    nc = l // t
    lanes = cb.shape[-1]
    dh = e // nheads
    chunk = lambda i, dd, s: (i, s + dd * (nc - 1 - 2 * s), 0)
    chunk_t = lambda i, dd, s: (i, 0, s + dd * (nc - 1 - 2 * s))
    chunk4 = lambda i, dd, s: (i, s + dd * (nc - 1 - 2 * s), 0, 0)
    late = lambda i, dd, s: (i, (nc - 1) - dd * s, 0)
    vec = lambda i, dd, s: (0, 0)
    final = final_g is not None
    extra_specs = [pl.BlockSpec((1, d), vec)] if final else []
    extra_args = [final_g.reshape(1, d)] if final else []
    return pl.pallas_call(
        functools.partial(_ml_main_kernel, nc=nc, nh=nheads, final=final),
        grid=(b, 2, nc),
        in_specs=[pl.BlockSpec((1, t, e), chunk),
                  pl.BlockSpec((1, e, t), chunk_t),
                  pl.BlockSpec((1, t, e), chunk),
                  pl.BlockSpec((1, t, lanes), chunk),
                  pl.BlockSpec((1, t, lanes), chunk),
                  pl.BlockSpec((1, 1, lanes, t), chunk4),
                  pl.BlockSpec((1, 1, lanes, t), chunk4),
                  pl.BlockSpec((1, t, e), late),
                  pl.BlockSpec((1, t, e), late),
                  pl.BlockSpec((1, t, d), late),
                  pl.BlockSpec((1, 1, d), lambda i, dd, s: (i, 0, 0)),
                  pl.BlockSpec((1, e), vec),
                  pl.BlockSpec((1, e), vec),
                  pl.BlockSpec((e, d), vec)] + extra_specs,
        out_specs=pl.BlockSpec((1, t, d), late),
        out_shape=_sds((b, l, d), F32),
        scratch_shapes=[pltpu.VMEM((nheads, dh, dh), F32),
                        pltpu.VMEM((nheads, 1, dh), F32),
                        pltpu.VMEM((nheads, 1, V7X_LANES), F32),
                        pltpu.VMEM((l, e), F32)],
        compiler_params=_params("arbitrary", "arbitrary", "arbitrary"),
        name="ml_main",
    )(q, kt, v, cb, gg, at, gt, xc, z, x, gate, gn_w.reshape(1, e), skip.reshape(1, e), w_out.astype(BF16),
      *extra_args)


def _ml_layer(x, shift, scale, gate, g, w_in, conv_w, conv_b, w_q, w_k, w_v, w_gates, b_gates,
              gn_w, skip, w_out, final_g):
    assert x.shape[1] % ML_CHUNK == 0
    nheads = b_gates.shape[0] // 4
    xm, z = _ml_pre(x, shift, scale, g, w_in.astype(BF16))
    q, kt, v, xc, gp = _ml_qkv(xm, conv_w, conv_b, w_q, w_k, w_v, w_gates, b_gates)
    cb, gg, at, gt = _ml_gates(gp, nheads)
    return _ml_main(q, kt, v, cb, gg, at, gt, xc, z, x, gate, gn_w, skip, w_out, nheads, final_g)


def _final_norm_kernel(x_ref, g_ref, o_ref):
    o_ref[0] = _rms(x_ref[0], g_ref[...])


def _final_norm(x, g):
    b, l, d = x.shape
    tl = min(TOKEN_TILE, l)
    return pl.pallas_call(
        _final_norm_kernel,
        grid=(b, l // tl),
        in_specs=[pl.BlockSpec((1, tl, d), lambda i, j: (i, j, 0)),
                  pl.BlockSpec((1, d), lambda i, j: (0, 0))],
        out_specs=pl.BlockSpec((1, tl, d), lambda i, j: (i, j, 0)),
        out_shape=_sds((b, l, d), F32),
        compiler_params=_params("arbitrary", "arbitrary"),
        name="final_norm",
    )(x, g.reshape(1, d))


def kernel(x, c, ada_w, ada_b, norm_g, s5_w_in, s5_lam_re, s5_lam_im, s5_log_dt, s5_b_re, s5_b_im, s5_c_re, s5_c_im, s5_d, s5_w_glu, s5_b_glu, s5_w_out, ml_w_in, ml_conv_w, ml_conv_b, ml_w_q, ml_w_k, ml_w_v, ml_w_gates, ml_b_gates, ml_gn_w, ml_skip, ml_w_out, final_g):
    depth = ada_w.shape[0]
    d = x.shape[-1]
    mod = _ada(c, ada_w, ada_b)
    for i in range(depth):
        shift = mod[i, :, None, 0:d]
        scale = mod[i, :, None, d:2 * d]
        gate = mod[i, :, None, 2 * d:3 * d]
        g = norm_g[i].reshape(1, d)
        j = i // 2
        if i % 2 == 0:
            x = _s5_layer(x, shift, scale, gate, g, s5_w_in[j], s5_lam_re[j], s5_lam_im[j], s5_log_dt[j],
                          s5_b_re[j], s5_b_im[j], s5_c_re[j], s5_c_im[j], s5_d[j], s5_w_glu[j],
                          s5_b_glu[j], s5_w_out[j])
        else:
            x = _ml_layer(x, shift, scale, gate, g, ml_w_in[j], ml_conv_w[j], ml_conv_b[j], ml_w_q[j],
                          ml_w_k[j], ml_w_v[j], ml_w_gates[j], ml_b_gates[j], ml_gn_w[j], ml_skip[j],
                          ml_w_out[j], final_g if i == depth - 1 else None)
    return x if depth % 2 == 0 else _final_norm(x, final_g)
```

```python
import functools

import jax
import jax.numpy as jnp
from jax import lax
from jax.experimental import pallas as pl
from jax.experimental.pallas import tpu as pltpu

F32 = jnp.float32
BF16 = jnp.bfloat16
U32 = jnp.uint32
HIGHEST = lax.Precision.HIGHEST

RMS_EPS = 1e-6
LN_EPS = 1e-5
V7X_LANES = 128
V7X_SUBLANES = 8
V7X_MXU_DIM = 256
VMEM_LIMIT_BYTES = 56 * 1024 * 1024
S5_CHUNK = V7X_LANES
S5_TILE = V7X_SUBLANES * S5_CHUNK
S5_SUB = 512
ML_CHUNK = V7X_MXU_DIM
TOKEN_TILE = 512


def _params(*sem):
    return pltpu.CompilerParams(dimension_semantics=sem, vmem_limit_bytes=VMEM_LIMIT_BYTES)


def _sds(shape, dtype):
    return jax.ShapeDtypeStruct(shape, dtype)


def _sigmoid(v):
    return 0.5 * jnp.tanh(0.5 * v) + 0.5


def _silu(v):
    return v * _sigmoid(v)


def _nt_dot(a, b, precision=None):
    return lax.dot_general(a, b, (((1,), (1,)), ((), ())), preferred_element_type=F32,
                           precision=precision)


def _cat(parts, axis):
    return parts[0] if len(parts) == 1 else jnp.concatenate(parts, axis=axis)


def _rms(x, g):
    ms = jnp.mean(x * x, axis=-1, keepdims=True)
    return x * lax.rsqrt(ms + RMS_EPS) * g


def _modulated_norm(x, g, shift, scale):
    return _rms(x, g) * (1.0 + scale) + shift


def _ada_kernel(c_ref, w_ref, b_ref, o_ref):
    sc = _silu(c_ref[...])
    o_ref[0] = jnp.dot(sc, w_ref[0], preferred_element_type=F32, precision=HIGHEST) + b_ref[0]


def _ada(c, ada_w, ada_b):
    depth, d, d3 = ada_w.shape
    b = c.shape[0]
    return pl.pallas_call(
        _ada_kernel,
        grid=(depth, d3 // d),
        in_specs=[pl.BlockSpec((b, d), lambda i, j: (0, 0)),
                  pl.BlockSpec((1, d, d), lambda i, j: (i, 0, j)),
                  pl.BlockSpec((1, 1, d), lambda i, j: (i, 0, j))],
        out_specs=pl.BlockSpec((1, b, d), lambda i, j: (i, 0, j)),
        out_shape=_sds((depth, b, d3), F32),
        compiler_params=_params("arbitrary", "arbitrary"),
        name="ada_mod",
    )(c, ada_w, ada_b.reshape(depth, 1, d3))


def _relayout_pitch(e):
    pitch = e + V7X_SUBLANES
    assert pitch % (2 * V7X_SUBLANES) == V7X_SUBLANES
    return pitch


def _s5_pre_kernel(x_ref, shift_ref, scale_ref, g_ref, wt_ref, ut_ref, uzt_ref, rl_scr, *, e, t, sub, pitch):
    kk = sub // t
    for s in range(x_ref.shape[1] // sub):
        h = _modulated_norm(x_ref[0, s * sub:(s + 1) * sub, :], g_ref[...], shift_ref[0], scale_ref[0])
        uz = _nt_dot(wt_ref[...], h.astype(BF16))
        uzt_ref[0, :, s * sub:(s + 1) * sub] = uz.astype(BF16)
        for k in range(kk):
            c = s * kk + k
            rl_scr[c * pitch:c * pitch + e, :] = uz[:e, k * t:(k + 1) * t]

    for ch in range(e):
        ut_ref[ch] = rl_scr[pl.ds(ch, V7X_SUBLANES, stride=pitch), :]


def _s5_pre(x, shift, scale, g, w_in_t):
    b, l, d = x.shape
    e = w_in_t.shape[0] // 2
    t, tl = S5_CHUNK, S5_TILE
    steps = l // tl
    k = tl // t
    pitch = _relayout_pitch(e)
    return pl.pallas_call(
        functools.partial(_s5_pre_kernel, e=e, t=t, sub=min(S5_SUB, tl), pitch=pitch),
        grid=(b, steps),
        in_specs=[pl.BlockSpec((1, tl, d), lambda i, j: (i, j, 0)),
                  pl.BlockSpec((1, 1, d), lambda i, j: (i, 0, 0)),
                  pl.BlockSpec((1, 1, d), lambda i, j: (i, 0, 0)),
                  pl.BlockSpec((1, d), lambda i, j: (0, 0)),
                  pl.BlockSpec((2 * e, d), lambda i, j: (0, 0))],
        out_specs=[pl.BlockSpec((e, k, t), lambda i, j: (0, i * steps + j, 0)),
                   pl.BlockSpec((1, 2 * e, tl), lambda i, j: (i, 0, j))],
        out_shape=[_sds((e, b * l // t, t), F32), _sds((b, 2 * e, l), BF16)],
        scratch_shapes=[pltpu.VMEM((k * pitch, t), F32)],
        compiler_params=_params("arbitrary", "arbitrary"),
        name="s5_pre",
    )(x, shift, scale, g, w_in_t)


def _abar(lre, lim, ldt):
    dt = jnp.exp(ldt)
    mag = jnp.exp(lre * dt)
    are = mag * jnp.cos(lim * dt)
    aim = mag * jnp.sin(lim * dt)
    den = lre * lre + lim * lim
    zre = ((are - 1.0) * lre + aim * lim) / den
    zim = (aim * lre - (are - 1.0) * lim) / den
    return are, aim, zre, zim


def _cplx_pow(ar, ai, expo, nbits):
    shape = expo.shape
    rr = jnp.ones(shape, F32)
    ri = jnp.zeros(shape, F32)
    pr, pi = ar, ai
    for k in range(nbits):
        bit = ((expo >> k) & 1) == 1
        mr = jnp.where(bit, pr, 1.0)
        mi = jnp.where(bit, pi, 0.0)
        rr, ri = rr * mr - ri * mi, rr * mi + ri * mr
        if k + 1 < nbits:
            pr, pi = pr * pr - pi * pi, 2.0 * pr * pi
    return rr, ri


def _s5_tables(prm, xmat_scr, ymat_scr, adec_scr, kw_scr, *, t, n, p):
    lre, lim, ldt, btre, btim, cre, cim = prm
    n2 = 2 * n
    nbits = (t - 1).bit_length()
    lane_row = lax.broadcasted_iota(jnp.int32, (1, n2), 1)
    lane_tn = lax.broadcasted_iota(jnp.int32, (t, n2), 1)
    sub_tn = lax.broadcasted_iota(jnp.int32, (t, n2), 0)
    lane_p = lax.broadcasted_iota(jnp.int32, (p, n2), 1)
    lane_pp = lax.broadcasted_iota(jnp.int32, (p * p, n2), 1)
    first_row = lane_row < n
    first_tn = lane_tn < n

    k_halves = []
    k0_bwd = None
    for d in range(2):
        are, aim, zre, zim = _abar(lre[d], lim[d], ldt[d])

        pr, pi = are, aim
        for _ in range(t.bit_length() - 1):
            pr, pi = pr * pr - pi * pi, 2.0 * pr * pi
        adec_scr[2 * d:2 * d + 1, :] = pr
        adec_scr[2 * d + 1:2 * d + 2, :] = jnp.where(first_row, -pi, pi)

        bbt_re = zre * btre[d] - zim * btim[d]
        bbt_im = zre * btim[d] + zim * btre[d]

        asc_re, asc_im = _cplx_pow(are, aim, sub_tn, nbits)
        dsc_re, dsc_im = _cplx_pow(are, aim, t - 1 - sub_tn, nbits)
        up_re, up_im = (asc_re, asc_im) if d == 0 else (dsc_re, dsc_im)
        up_re, up_im = up_re * are - up_im * aim, up_re * aim + up_im * are

        rows = []
        for pp in range(p):
            c_re = cre[d][pp:pp + 1, :]
            c_im = cim[d][pp:pp + 1, :]
            cb_re = c_re * bbt_re - c_im * bbt_im
            cb_im = c_re * bbt_im + c_im * bbt_re
            rows.append(jnp.where(lane_p < n, cb_re, -cb_im))
        cbw = jnp.concatenate(rows, axis=0)
        lag_re, lag_im = (asc_re, asc_im) if d == 0 else (up_re, up_im)
        k_halves.append(_nt_dot(cbw, jnp.where(first_tn, lag_re, lag_im), precision=HIGHEST))
        if d == 1:
            k0_bwd = jnp.sum(jnp.where(lane_pp < n, cbw, 0.0), axis=1, keepdims=True)

        xt_re, xt_im = (dsc_re, dsc_im) if d == 0 else (asc_re, asc_im)
        for q in range(p):
            b_re = bbt_re[q:q + 1, :]
            b_im = bbt_im[q:q + 1, :]
            v1 = jnp.where(first_row, b_re, b_im)
            v2 = jnp.where(first_row, b_im, -b_re)
            xmat_scr[q * t:(q + 1) * t, d * n2:(d + 1) * n2] = (xt_re * v1 - xt_im * v2).astype(BF16)

        ya = jnp.where(first_tn, up_re, -up_im)
        yb = jnp.where(first_tn, up_im, up_re)
        for pp in range(p):
            ymat_scr[pp * t:(pp + 1) * t, d * n2:(d + 1) * n2] = (
                cre[d][pp:pp + 1, :] * ya - cim[d][pp:pp + 1, :] * yb).astype(BF16)

    lane_k = lax.broadcasted_iota(jnp.int32, (p * p, t), 1)
    kfull = jnp.concatenate([k_halves[0] + jnp.where(lane_k == 0, k0_bwd, 0.0), k_halves[1]], axis=1)
    kprev = pltpu.roll(kfull, 1, 1)
    lo = pltpu.bitcast(kfull.astype(BF16).astype(F32), U32) >> 16
    hi = pltpu.bitcast(kprev.astype(BF16).astype(F32), U32) & jnp.uint32(0xFFFF0000)
    words = hi | lo
    kw_scr[0] = words[:, :t]
    kw_scr[1] = words[:, t:]


def _s5_wrap_mask(t):
    wr = lax.broadcasted_iota(jnp.int32, (t // 2, t), 0)
    ln = lax.broadcasted_iota(jnp.int32, (t // 2, t), 1)
    return ln + 2 * wr <= t - 1


def _s5_toep_tile(kw_scr, toep_scr, q, pp, wrap_mask, *, t, p):
    r = pp * p + q
    full = jnp.where(wrap_mask, kw_scr[0, pl.ds(r, 1), :], kw_scr[1, pl.ds(r, 1), :])
    rolled = pltpu.roll(full, 0, 1, stride=2, stride_axis=0)
    start = q * t if isinstance(q, int) else pl.multiple_of(q * t, t)
    toep_scr[pl.ds(start, t), pp * t:(pp + 1) * t] = pltpu.bitcast(rolled, BF16)


def _s5_ssm_kernel(u_ref, *refs, nc, p, t, n):
    cur_prm, nxt_prm, y_ref = refs[0:7], refs[7:14], refs[14]
    toep_scr, xmat_scr, ymat_scr, adec_scr, kw_scr, lhs_scr, xs_scr, xw_scr, sin_scr = refs[15:]
    n2 = 2 * n
    bsz = u_ref.shape[1] // nc
    g = pl.program_id(0)
    slot = lax.rem(g, 2)
    nslot = 1 - slot
    kw = dict(t=t, n=n, p=p)
    wrap_mask = _s5_wrap_mask(t)

    @pl.when(g == 0)
    def _():
        _s5_tables(cur_prm, xmat_scr.at[0], ymat_scr.at[0], adec_scr.at[0], kw_scr.at[0], **kw)

        def q_body(q, carry):
            for pp in range(p):
                _s5_toep_tile(kw_scr.at[0], toep_scr.at[0], q, pp, wrap_mask, t=t, p=p)
            return carry

        lax.fori_loop(0, p, q_body, 0)

    _s5_tables(nxt_prm, xmat_scr.at[nslot], ymat_scr.at[nslot], adec_scr.at[nslot], kw_scr.at[nslot], **kw)

    for q in range(p):
        lhs_scr[:, q * t:(q + 1) * t] = u_ref[q].astype(BF16)
    lhs = lhs_scr[...]

    xmat = xmat_scr.at[slot]
    ymat = ymat_scr.at[slot]
    adec = adec_scr.at[slot]
    toep = toep_scr.at[slot]
    xs_both = jnp.dot(lhs, xmat[...], preferred_element_type=F32)
    for d in range(2):
        xs = xs_both[:, d * n2:(d + 1) * n2]
        xs_scr[d] = xs
        xw_scr[d] = pltpu.roll(xs, n, 1)

    for d in range(2):
        a1 = adec[2 * d:2 * d + 1, :]
        a2 = adec[2 * d + 1:2 * d + 2, :]
        s = jnp.zeros((bsz, n2), F32)
        sw = jnp.zeros((bsz, n2), F32)
        for c in (range(nc) if d == 0 else reversed(range(nc))):
            rows = pl.ds(c, bsz, stride=nc)
            sin_scr[d, rows, :] = s
            s, sw = (a1 * s + a2 * sw + xs_scr[d, rows, :],
                     a1 * sw - a2 * s + xw_scr[d, rows, :])

    sin_both = jnp.concatenate([sin_scr[0].astype(BF16), sin_scr[1].astype(BF16)], axis=1)
    w = V7X_MXU_DIM
    npanels = p * t // w
    tiles = [(q, pp) for q in range(p) for pp in range(p)]
    per_panel = len(tiles) // npanels
    for pn in range(npanels):
        cols = slice(pn * w, (pn + 1) * w)
        acc = jnp.dot(lhs, toep[:, cols], preferred_element_type=F32)
        acc = acc + _nt_dot(sin_both, ymat[cols, :])
        for h in range(w // t):
            y_ref[pn * (w // t) + h] = acc[:, h * t:(h + 1) * t]
        for q, pp in tiles[pn * per_panel:(pn + 1) * per_panel]:
            _s5_toep_tile(kw_scr.at[nslot], toep_scr.at[nslot], q, pp, wrap_mask, t=t, p=p)


def _s5_ssm(ut, nc, lam_re, lam_im, log_dt, b_re, b_im, c_re, c_im):
    e, bc, t = ut.shape
    _, g, n, p = b_re.shape
    n2 = 2 * n
    pt = p * t
    dup = lambda a: jnp.concatenate([a, a], axis=-1)
    lre = dup(lam_re)[:, :, None, :]
    lim = dup(lam_im)[:, :, None, :]
    ldt = log_dt[:, :, None, None]
    btre = dup(jnp.swapaxes(b_re, -1, -2))
    btim = dup(jnp.swapaxes(b_im, -1, -2))
    cre = dup(c_re)
    cim = dup(c_im)
    prm = (lre, lim, ldt, btre, btim, cre, cim)

    def this_group(a):
        return pl.BlockSpec((2, None) + a.shape[2:], lambda i: (0, i, 0, 0))

    def next_group(a):
        return pl.BlockSpec((2, None) + a.shape[2:], lambda i: (0, jnp.minimum(i + 1, g - 1), 0, 0))

    return pl.pallas_call(
        functools.partial(_s5_ssm_kernel, nc=nc, p=p, t=t, n=n),
        grid=(g,),
        in_specs=([pl.BlockSpec((p, bc, t), lambda i: (i, 0, 0))]
                  + [this_group(a) for a in prm] + [next_group(a) for a in prm]),
        out_specs=pl.BlockSpec((p, bc, t), lambda i: (i, 0, 0)),
        out_shape=_sds((e, bc, t), F32),
        scratch_shapes=[pltpu.VMEM((2, pt, pt), BF16),
                        pltpu.VMEM((2, pt, 2 * n2), BF16),
                        pltpu.VMEM((2, pt, 2 * n2), BF16),
                        pltpu.VMEM((2, 4, n2), F32),
                        pltpu.VMEM((2, 2, p * p, t), U32),
                        pltpu.VMEM((bc, pt), BF16),
                        pltpu.VMEM((2, bc, n2), F32),
                        pltpu.VMEM((2, bc, n2), F32),
                        pltpu.VMEM((2, bc, n2), F32)],
        compiler_params=_params("arbitrary"),
        name="s5_ssm",
    )(ut, *prm, *prm)


def _s5_post_kernel(y_ref, uzt_ref, x_ref, gate_ref, dcol_ref, bcol_ref, wgt_ref, wot_ref,
                    o_ref, rl_scr, *, e, t, sub, pitch):
    for ch in range(e):
        rl_scr[pl.ds(ch, V7X_SUBLANES, stride=pitch), :] = y_ref[ch]

    k = sub // t
    reps = sub // dcol_ref.shape[1]
    dfull = _cat([dcol_ref[...]] * reps, 1)
    bfull = _cat([bcol_ref[...]] * reps, 1)
    for s in range(x_ref.shape[1] // sub):
        tok = slice(s * sub, (s + 1) * sub)
        yt = _cat([rl_scr[(s * k + kk) * pitch:(s * k + kk) * pitch + e, :] for kk in range(k)], 1)
        ut = uzt_ref[0, :e, tok].astype(F32)
        y = jax.nn.gelu(yt + dfull * ut)
        g = jnp.dot(wgt_ref[...], y.astype(BF16), preferred_element_type=F32) + bfull
        y = y * _sigmoid(g)
        z = uzt_ref[0, e:, tok].astype(F32)
        yz = y * _silu(z)
        ot = jnp.dot(wot_ref[...], yz.astype(BF16), preferred_element_type=F32)
        o_ref[0, tok, :] = x_ref[0, tok, :] + gate_ref[0] * ot.T


def _s5_post(yt, uzt, x, gate, d_skip, b_glu, w_glu_t, w_out_t):
    b, l, d = x.shape
    e, bc, t = yt.shape
    tl = S5_TILE
    k = tl // t
    lanes = V7X_LANES
    dcol = jnp.broadcast_to(d_skip[:, None], (e, lanes))
    bcol = jnp.broadcast_to(b_glu[:, None], (e, lanes))
    steps = l // tl
    pitch = _relayout_pitch(e)
    return pl.pallas_call(
        functools.partial(_s5_post_kernel, e=e, t=t, sub=min(S5_SUB, tl), pitch=pitch),
        grid=(b, steps),
        in_specs=[pl.BlockSpec((e, k, t), lambda i, j: (0, i * steps + j, 0)),
                  pl.BlockSpec((1, 2 * e, tl), lambda i, j: (i, 0, j)),
                  pl.BlockSpec((1, tl, d), lambda i, j: (i, j, 0)),
                  pl.BlockSpec((1, 1, d), lambda i, j: (i, 0, 0)),
                  pl.BlockSpec((e, lanes), lambda i, j: (0, 0)),
                  pl.BlockSpec((e, lanes), lambda i, j: (0, 0)),
                  pl.BlockSpec((e, e), lambda i, j: (0, 0)),
                  pl.BlockSpec((d, e), lambda i, j: (0, 0))],
        out_specs=pl.BlockSpec((1, tl, d), lambda i, j: (i, j, 0)),
        out_shape=_sds((b, l, d), F32),
        scratch_shapes=[pltpu.VMEM((k * pitch, t), F32)],
        compiler_params=_params("arbitrary", "arbitrary"),
        name="s5_post",
    )(yt, uzt, x, gate, dcol, bcol, w_glu_t, w_out_t)


def _s5_layer(x, shift, scale, gate, g, w_in, lam_re, lam_im, log_dt, b_re, b_im, c_re, c_im,
              d_skip, w_glu, b_glu, w_out):
    l = x.shape[1]
    assert l % S5_TILE == 0
    ut, uzt = _s5_pre(x, shift, scale, g, w_in.T.astype(BF16))
    yt = _s5_ssm(ut, l // S5_CHUNK, lam_re, lam_im, log_dt, b_re, b_im, c_re, c_im)
    return _s5_post(yt, uzt, x, gate, d_skip, b_glu, w_glu.T.astype(BF16), w_out.T.astype(BF16))


def _ml_pre_kernel(x_ref, shift_ref, scale_ref, g_ref, w_ref, xm_ref, z_ref, *, e):
    h = _modulated_norm(x_ref[0], g_ref[...], shift_ref[0], scale_ref[0])
    xz = jnp.dot(h.astype(BF16), w_ref[...], preferred_element_type=F32)
    xm_ref[0] = xz[:, :e].astype(BF16)
    z_ref[0] = xz[:, e:].astype(BF16)


def _ml_pre(x, shift, scale, g, w_in):
    b, l, d = x.shape
    e = w_in.shape[1] // 2
    tl = min(TOKEN_TILE, l)
    return pl.pallas_call(
        functools.partial(_ml_pre_kernel, e=e),
        grid=(b, l // tl),
        in_specs=[pl.BlockSpec((1, tl, d), lambda i, j: (i, j, 0)),
                  pl.BlockSpec((1, 1, d), lambda i, j: (i, 0, 0)),
                  pl.BlockSpec((1, 1, d), lambda i, j: (i, 0, 0)),
                  pl.BlockSpec((1, d), lambda i, j: (0, 0)),
                  pl.BlockSpec((d, 2 * e), lambda i, j: (0, 0))],
        out_specs=[pl.BlockSpec((1, tl, e), lambda i, j: (i, j, 0)),
                   pl.BlockSpec((1, tl, e), lambda i, j: (i, j, 0))],
        out_shape=[_sds((b, l, e), BF16), _sds((b, l, e), BF16)],
        compiler_params=_params("arbitrary", "arbitrary"),
        name="ml_pre",
    )(x, shift, scale, g, w_in)


HALO = 2 * V7X_SUBLANES


def _ml_qkv_kernel(xm_ref, prev_ref, next_ref, cw_ref, cb_ref, wq_ref, wk_ref, wv_ref, wg_ref,
                   bg_ref, q_ref, kt_ref, v_ref, xc_ref, gp_ref, *, steps, taps):
    j = pl.program_id(1)
    xm_b = xm_ref[0]
    tq, e = xm_b.shape
    zero = jnp.zeros((), BF16)
    prev = jnp.where(j == 0, zero, prev_ref[0])
    nxt = jnp.where(j == steps - 1, zero, next_ref[0])
    xe = jnp.concatenate([prev, xm_b, nxt], axis=0).astype(F32)
    pad = taps // 2
    acc = jnp.broadcast_to(cb_ref[...], (tq, e))
    for kk in range(taps):
        off = HALO + kk - pad
        acc = acc + xe[off:off + tq, :] * cw_ref[kk:kk + 1, :]
    xc = _silu(acc)
    xc_b = xc.astype(BF16)
    tile = wq_ref.shape[-1]
    qs, ks, vs = [], [], []
    for i in range(e // tile):
        cols = slice(i * tile, (i + 1) * tile)
        qs.append(jnp.dot(xc_b[:, cols], wq_ref[i], preferred_element_type=F32))
        ks.append(jnp.dot(xc_b[:, cols], wk_ref[i], preferred_element_type=F32).astype(BF16))
        vs.append(jnp.dot(xm_b[:, cols], wv_ref[i], preferred_element_type=F32))
    q = jnp.concatenate(qs, axis=1).astype(BF16)
    k = jnp.concatenate(ks, axis=1)
    v = jnp.concatenate(vs, axis=1).astype(BF16)
    gp = (jnp.dot(q, wg_ref[0], preferred_element_type=F32)
          + jnp.dot(k, wg_ref[1], preferred_element_type=F32)
          + jnp.dot(v, wg_ref[2], preferred_element_type=F32) + bg_ref[...])
    q_ref[0] = q
    for i in range(e // tile):
        kt_ref[0, i * tile:(i + 1) * tile, :] = ks[i].T
    v_ref[0] = v
    xc_ref[0] = xc_b
    gp_ref[0] = gp


def _blockdiag_tiles(w, tile):
    nblk, bs, _ = w.shape
    rows = w.reshape(nblk // (tile // bs), tile, bs)
    spread = jnp.tile(rows, (1, 1, tile // bs))
    blk = jnp.arange(tile) // bs
    return jnp.where(blk[:, None] == blk[None, :], spread, 0.0)


def _ml_qkv(xm, conv_w, conv_b, w_q, w_k, w_v, w_gates, b_gates):
    b, l, e = xm.shape
    tq = min(TOKEN_TILE, l)
    steps = l // tq
    taps = conv_w.shape[0]
    tile = V7X_MXU_DIM
    lanes = V7X_LANES
    ng = w_gates.shape[-1]
    wq = _blockdiag_tiles(w_q, tile).astype(BF16)
    wk = _blockdiag_tiles(w_k, tile).astype(BF16)
    wv = _blockdiag_tiles(w_v, tile).astype(BF16)
    wg = jnp.pad(w_gates, ((0, 0), (0, 0), (0, lanes - ng))).astype(BF16)
    bg = jnp.pad(b_gates, (0, lanes - ng)).reshape(1, lanes)
    hb = tq // HALO
    last = l // HALO - 1
    tok = lambda i, j: (i, j, 0)
    full3 = lambda i, j: (0, 0, 0)
    full2 = lambda i, j: (0, 0)
    return pl.pallas_call(
        functools.partial(_ml_qkv_kernel, steps=steps, taps=taps),
        grid=(b, steps),
        in_specs=[pl.BlockSpec((1, tq, e), tok),
                  pl.BlockSpec((1, HALO, e), lambda i, j: (i, jnp.maximum(j * hb - 1, 0), 0)),
                  pl.BlockSpec((1, HALO, e), lambda i, j: (i, jnp.minimum((j + 1) * hb, last), 0)),
                  pl.BlockSpec((taps, e), full2),
                  pl.BlockSpec((1, e), full2),
                  pl.BlockSpec(wq.shape, full3),
                  pl.BlockSpec(wk.shape, full3),
                  pl.BlockSpec(wv.shape, full3),
                  pl.BlockSpec(wg.shape, full3),
                  pl.BlockSpec((1, lanes), full2)],
        out_specs=[pl.BlockSpec((1, tq, e), tok), pl.BlockSpec((1, e, tq), lambda i, j: (i, 0, j)),
                   pl.BlockSpec((1, tq, e), tok), pl.BlockSpec((1, tq, e), tok),
                   pl.BlockSpec((1, tq, lanes), tok)],
        out_shape=[_sds((b, l, e), BF16), _sds((b, e, l), BF16), _sds((b, l, e), BF16),
                   _sds((b, l, e), BF16), _sds((b, l, lanes), F32)],
        compiler_params=_params("arbitrary", "arbitrary"),
        name="ml_qkv",
    )(xm, xm, xm, conv_w, conv_b.reshape(1, e), wq, wk, wv, wg, bg)


def _ml_gates_kernel(gp_ref, cb_ref, gg_ref, at_ref, gt_ref, *, nf, t):
    lanes = gp_ref.shape[2]
    ri = lax.broadcasted_iota(jnp.int32, (t, t), 0)
    ci = lax.broadcasted_iota(jnp.int32, (t, t), 1)
    lower = jnp.where(ci <= ri, 1.0, 0.0).astype(BF16)
    upper = jnp.where(ci >= ri, 1.0, 0.0).astype(BF16)

    def tri_sums(tri, parts):
        return sum(jnp.dot(tri, part, preferred_element_type=F32) for part in parts)

    fwd = lax.broadcasted_iota(jnp.int32, (t, lanes), 1) < 2 * nf
    fwd_row = lax.broadcasted_iota(jnp.int32, (1, lanes), 1) < 2 * nf
    for c in range(gp_ref.shape[1] // t):
        rows = slice(c * t, (c + 1) * t)
        x = gp_ref[0, rows, :]
        lf = jax.nn.log_sigmoid(x)
        ipre = pltpu.roll(x, nf, 1)
        hi = lf.astype(BF16)
        rest = lf - hi.astype(F32)
        mid = rest.astype(BF16)
        low = (rest - mid.astype(F32)).astype(BF16)
        cum_f = tri_sums(lower, (hi, mid, low))
        cum_b = tri_sums(upper, (hi, mid, low))
        cum = jnp.where(fwd, cum_f, cum_b)
        tot = jnp.where(fwd_row, cum_f[t - 1:t, :], cum_b[0:1, :])
        gg = tot - cum + ipre
        cb_ref[0, rows, :] = cum
        gg_ref[0, rows, :] = gg
        at_ref[0, c] = (ipre - cum).T
        gt_ref[0, c] = gg.T


def _ml_gates(gp, nheads):
    b, l, lanes = gp.shape
    t = ML_CHUNK
    nc = l // t
    return pl.pallas_call(
        functools.partial(_ml_gates_kernel, nf=nheads, t=t),
        grid=(b,),
        in_specs=[pl.BlockSpec((1, l, lanes), lambda i: (i, 0, 0))],
        out_specs=[pl.BlockSpec((1, l, lanes), lambda i: (i, 0, 0)),
                   pl.BlockSpec((1, l, lanes), lambda i: (i, 0, 0)),
                   pl.BlockSpec((1, nc, lanes, t), lambda i: (i, 0, 0, 0)),
                   pl.BlockSpec((1, nc, lanes, t), lambda i: (i, 0, 0, 0))],
        out_shape=[_sds((b, l, lanes), F32), _sds((b, l, lanes), F32), _sds((b, nc, lanes, t), F32),
                   _sds((b, nc, lanes, t), F32)],
        compiler_params=_params("arbitrary"),
        name="ml_gates",
    )(gp)


def _ml_main_kernel(q_ref, kt_ref, v_ref, cb_ref, gg_ref, at_ref, gt_ref, xc_ref, z_ref, x_ref, gate_ref,
                    gnw_ref, skip_ref, wout_ref, *rest, nc, nh, final):
    if final:
        fg_ref, o_ref, c_scr, n_scr, m_scr, hf_scr = rest
    else:
        o_ref, c_scr, n_scr, m_scr, hf_scr = rest
    dirn = pl.program_id(1)
    step = pl.program_id(2)
    t = q_ref.shape[1]
    dh = q_ref.shape[2] // nh
    scale = float(dh) ** -0.5

    @pl.when(step == 0)
    def _():
        c_scr[...] = jnp.zeros_like(c_scr)
        n_scr[...] = jnp.zeros_like(n_scr)
        m_scr[...] = jnp.zeros_like(m_scr)

    ri = lax.broadcasted_iota(jnp.int32, (t, t), 0)
    ci = lax.broadcasted_iota(jnp.int32, (t, t), 1)
    ones_rows = jnp.ones((V7X_SUBLANES, t), BF16)

    def heads_out(d):
        keep = (ci <= ri) if d == 0 else (ci >= ri)
        stage = []
        for h in range(nh):
            cols = slice(h * dh, (h + 1) * dh)
            gcol = d * 2 * nh + nh + h
            q = q_ref[0, :, cols]
            kt = kt_ref[0, cols, :]
            cb = cb_ref[0, :, gcol:gcol + 1]
            gg = gg_ref[0, :, gcol:gcol + 1]
            a_row = at_ref[0, 0, gcol:gcol + 1, :]
            g_row = gt_ref[0, 0, gcol:gcol + 1, :]
            tot = cb[t - 1:t, :] if d == 0 else cb[0:1, :]
            m = m_scr[h][:, 0:1]
            dmat = jnp.where(keep, cb + a_row, -jnp.inf)
            inter = cb + m
            m_t = jnp.maximum(inter, jnp.max(dmat, axis=1, keepdims=True))
            m_new = jnp.maximum(tot + m, jnp.max(gg, axis=0, keepdims=True))
            stage.append(dict(
                q=q, kt=kt, m_t=m_t, m_new=m_new,
                w_intra=jnp.exp(dmat - m_t),
                w_inter=jnp.exp(inter - m_t),
                qk=jnp.dot(q, kt, preferred_element_type=F32),
                decay=jnp.exp(tot + m - m_new),
                wk_row=jnp.exp(g_row - m_new) * scale))
        for h in range(nh):
            cols = slice(h * dh, (h + 1) * dh)
            a = stage[h]
            v = v_ref[0, :, cols]
            s = a['qk'] * scale * a['w_intra']
            qw = a['q'] * a['w_inter'].astype(BF16)
            num = jnp.dot(s.astype(BF16), v, preferred_element_type=F32)
            a['num'] = num + jnp.dot(qw, c_scr[h].astype(BF16), preferred_element_type=F32)
            qn = jnp.sum(a['q'].astype(F32) * n_scr[h], axis=1, keepdims=True)
            a['nq'] = jnp.sum(s, axis=1, keepdims=True) + a['w_inter'] * qn
        for h in range(nh):
            cols = slice(h * dh, (h + 1) * dh)
            a = stage[h]
            v = v_ref[0, :, cols]
            a['hout'] = a['num'] / jnp.maximum(jnp.abs(a['nq']), jnp.exp(-a['m_t']))
            kwt = a['kt'] * a['wk_row'].astype(BF16)
            a['c_new'] = a['decay'] * c_scr[h] + jnp.dot(kwt, v, preferred_element_type=F32)
            a['n_add'] = _nt_dot(ones_rows, kwt)[0:1, :]
        for h in range(nh):
            a = stage[h]
            c_scr[h] = a['c_new']
            n_scr[h] = a['decay'] * n_scr[h] + a['n_add']
            m_scr[h] = jnp.broadcast_to(a['m_new'], m_scr.shape[1:])
        return [a['hout'] for a in stage]

    @pl.when(dirn == 0)
    def _():
        rows = pl.ds(pl.multiple_of(step * t, t), t)
        for h, hout in enumerate(heads_out(0)):
            hf_scr[rows, h * dh:(h + 1) * dh] = hout

    @pl.when(dirn == 1)
    def _():
        rows = pl.ds(pl.multiple_of((nc - 1 - step) * t, t), t)
        acc = None
        for h, hout in enumerate(heads_out(1)):
            cols = slice(h * dh, (h + 1) * dh)
            hs = hf_scr[rows, cols] + hout
            mu = jnp.mean(hs, axis=1, keepdims=True)
            cen = hs - mu
            var = jnp.mean(cen * cen, axis=1, keepdims=True)
            hn = cen * lax.rsqrt(var + LN_EPS)
            y = hn * gnw_ref[:, cols] + skip_ref[:, cols] * xc_ref[0, :, cols].astype(F32)
            y = y * _silu(z_ref[0, :, cols].astype(F32))
            part = jnp.dot(y.astype(BF16), wout_ref[cols, :], preferred_element_type=F32)
            acc = part if acc is None else acc + part
        out = x_ref[0] + gate_ref[0] * acc
        o_ref[0] = _rms(out, fg_ref[...]) if final else out


def _ml_main(q, kt, v, cb, gg, at, gt, xc, z, x, gate, gn_w, skip, w_out, nheads, final_g):
    b, l, e = q.shape
    d = x.shape[-1]
    t = ML_CHUNK
    nc = l // t
    lanes = cb.shape[-1]
    dh = e // nheads
    chunk = lambda i, dd, s: (i, s + dd * (nc - 1 - 2 * s), 0)
    chunk_t = lambda i, dd, s: (i, 0, s + dd * (nc - 1 - 2 * s))
    chunk4 = lambda i, dd, s: (i, s + dd * (nc - 1 - 2 * s), 0, 0)
    late = lambda i, dd, s: (i, (nc - 1) - dd * s, 0)
    vec = lambda i, dd, s: (0, 0)
    final = final_g is not None
    extra_specs = [pl.BlockSpec((1, d), vec)] if final else []
    extra_args = [final_g.reshape(1, d)] if final else []
    return pl.pallas_call(
        functools.partial(_ml_main_kernel, nc=nc, nh=nheads, final=final),
        grid=(b, 2, nc),
        in_specs=[pl.BlockSpec((1, t, e), chunk),
                  pl.BlockSpec((1, e, t), chunk_t),
                  pl.BlockSpec((1, t, e), chunk),
                  pl.BlockSpec((1, t, lanes), chunk),
                  pl.BlockSpec((1, t, lanes), chunk),
                  pl.BlockSpec((1, 1, lanes, t), chunk4),
                  pl.BlockSpec((1, 1, lanes, t), chunk4),
                  pl.BlockSpec((1, t, e), late),
                  pl.BlockSpec((1, t, e), late),
                  pl.BlockSpec((1, t, d), late),
                  pl.BlockSpec((1, 1, d), lambda i, dd, s: (i, 0, 0)),
                  pl.BlockSpec((1, e), vec),
                  pl.BlockSpec((1, e), vec),
                  pl.BlockSpec((e, d), vec)] + extra_specs,
        out_specs=pl.BlockSpec((1, t, d), late),
        out_shape=_sds((b, l, d), F32),
        scratch_shapes=[pltpu.VMEM((nheads, dh, dh), F32),
                        pltpu.VMEM((nheads, 1, dh), F32),
                        pltpu.VMEM((nheads, 1, V7X_LANES), F32),
                        pltpu.VMEM((l, e), F32)],
        compiler_params=_params("arbitrary", "arbitrary", "arbitrary"),
        name="ml_main",
    )(q, kt, v, cb, gg, at, gt, xc, z, x, gate, gn_w.reshape(1, e), skip.reshape(1, e), w_out.astype(BF16),
      *extra_args)


def _ml_layer(x, shift, scale, gate, g, w_in, conv_w, conv_b, w_q, w_k, w_v, w_gates, b_gates,
              gn_w, skip, w_out, final_g):
    assert x.shape[1] % ML_CHUNK == 0
    nheads = b_gates.shape[0] // 4
    xm, z = _ml_pre(x, shift, scale, g, w_in.astype(BF16))
    q, kt, v, xc, gp = _ml_qkv(xm, conv_w, conv_b, w_q, w_k, w_v, w_gates, b_gates)
    cb, gg, at, gt = _ml_gates(gp, nheads)
    return _ml_main(q, kt, v, cb, gg, at, gt, xc, z, x, gate, gn_w, skip, w_out, nheads, final_g)


def _final_norm_kernel(x_ref, g_ref, o_ref):
    o_ref[0] = _rms(x_ref[0], g_ref[...])


def _final_norm(x, g):
    b, l, d = x.shape
    tl = min(TOKEN_TILE, l)
    return pl.pallas_call(
        _final_norm_kernel,
        grid=(b, l // tl),
        in_specs=[pl.BlockSpec((1, tl, d), lambda i, j: (i, j, 0)),
                  pl.BlockSpec((1, d), lambda i, j: (0, 0))],
        out_specs=pl.BlockSpec((1, tl, d), lambda i, j: (i, j, 0)),
        out_shape=_sds((b, l, d), F32),
        compiler_params=_params("arbitrary", "arbitrary"),
        name="final_norm",
    )(x, g.reshape(1, d))


def kernel(x, c, ada_w, ada_b, norm_g, s5_w_in, s5_lam_re, s5_lam_im, s5_log_dt, s5_b_re, s5_b_im, s5_c_re, s5_c_im, s5_d, s5_w_glu, s5_b_glu, s5_w_out, ml_w_in, ml_conv_w, ml_conv_b, ml_w_q, ml_w_k, ml_w_v, ml_w_gates, ml_b_gates, ml_gn_w, ml_skip, ml_w_out, final_g):
    depth = ada_w.shape[0]
    d = x.shape[-1]
    mod = _ada(c, ada_w, ada_b)
    for i in range(depth):
        shift = mod[i, :, None, 0:d]
        scale = mod[i, :, None, d:2 * d]
        gate = mod[i, :, None, 2 * d:3 * d]
        g = norm_g[i].reshape(1, d)
        j = i // 2
        if i % 2 == 0:
            x = _s5_layer(x, shift, scale, gate, g, s5_w_in[j], s5_lam_re[j], s5_lam_im[j], s5_log_dt[j],
                          s5_b_re[j], s5_b_im[j], s5_c_re[j], s5_c_im[j], s5_d[j], s5_w_glu[j],
                          s5_b_glu[j], s5_w_out[j])
        else:
            x = _ml_layer(x, shift, scale, gate, g, ml_w_in[j], ml_conv_w[j], ml_conv_b[j], ml_w_q[j],
                          ml_w_k[j], ml_w_v[j], ml_w_gates[j], ml_b_gates[j], ml_gn_w[j], ml_skip[j],
                          ml_w_out[j], final_g if i == depth - 1 else None)
    return x if depth % 2 == 0 else _final_norm(x, final_g)
```

```python
import functools

import jax
import jax.numpy as jnp
from jax import lax
from jax.experimental import pallas as pl
from jax.experimental.pallas import tpu as pltpu

F32 = jnp.float32
BF16 = jnp.bfloat16
U32 = jnp.uint32
HIGHEST = lax.Precision.HIGHEST

RMS_EPS = 1e-6
LN_EPS = 1e-5
V7X_LANES = 128
V7X_SUBLANES = 8
V7X_MXU_DIM = 256
VMEM_LIMIT_BYTES = 56 * 1024 * 1024
S5_CHUNK = V7X_LANES
S5_TILE = V7X_SUBLANES * S5_CHUNK
S5_SUB = 512
ML_CHUNK = V7X_MXU_DIM
TOKEN_TILE = 512


def _params(*sem):
    return pltpu.CompilerParams(dimension_semantics=sem, vmem_limit_bytes=VMEM_LIMIT_BYTES)


def _sds(shape, dtype):
    return jax.ShapeDtypeStruct(shape, dtype)


def _sigmoid(v):
    return 0.5 * jnp.tanh(0.5 * v) + 0.5


def _silu(v):
    return v * _sigmoid(v)


def _nt_dot(a, b, precision=None):
    return lax.dot_general(a, b, (((1,), (1,)), ((), ())), preferred_element_type=F32,
                           precision=precision)


def _cat(parts, axis):
    return parts[0] if len(parts) == 1 else jnp.concatenate(parts, axis=axis)


def _rms(x, g):
    ms = jnp.mean(x * x, axis=-1, keepdims=True)
    return x * lax.rsqrt(ms + RMS_EPS) * g


def _modulated_norm(x, g, shift, scale):
    return _rms(x, g) * (1.0 + scale) + shift


def _ada_kernel(c_ref, w_ref, b_ref, o_ref):
    sc = _silu(c_ref[...])
    o_ref[0] = jnp.dot(sc, w_ref[0], preferred_element_type=F32, precision=HIGHEST) + b_ref[0]


def _ada(c, ada_w, ada_b):
    depth, d, d3 = ada_w.shape
    b = c.shape[0]
    return pl.pallas_call(
        _ada_kernel,
        grid=(depth, d3 // d),
        in_specs=[pl.BlockSpec((b, d), lambda i, j: (0, 0)),
                  pl.BlockSpec((1, d, d), lambda i, j: (i, 0, j)),
                  pl.BlockSpec((1, 1, d), lambda i, j: (i, 0, j))],
        out_specs=pl.BlockSpec((1, b, d), lambda i, j: (i, 0, j)),
        out_shape=_sds((depth, b, d3), F32),
        compiler_params=_params("arbitrary", "arbitrary"),
        name="ada_mod",
    )(c, ada_w, ada_b.reshape(depth, 1, d3))


def _relayout_pitch(e):
    pitch = e + V7X_SUBLANES
    assert pitch % (2 * V7X_SUBLANES) == V7X_SUBLANES
    return pitch


def _s5_pre_kernel(x_ref, shift_ref, scale_ref, g_ref, wt_ref, ut_ref, uzt_ref, rl_scr, *, e, t, sub, pitch):
    kk = sub // t
    for s in range(x_ref.shape[1] // sub):
        h = _modulated_norm(x_ref[0, s * sub:(s + 1) * sub, :], g_ref[...], shift_ref[0], scale_ref[0])
        uz = _nt_dot(wt_ref[...], h.astype(BF16))
        uzt_ref[0, :, s * sub:(s + 1) * sub] = uz.astype(BF16)
        for k in range(kk):
            c = s * kk + k
            rl_scr[c * pitch:c * pitch + e, :] = uz[:e, k * t:(k + 1) * t]

    for ch in range(e):
        ut_ref[ch] = rl_scr[pl.ds(ch, V7X_SUBLANES, stride=pitch), :]


def _s5_pre(x, shift, scale, g, w_in_t):
    b, l, d = x.shape
    e = w_in_t.shape[0] // 2
    t, tl = S5_CHUNK, S5_TILE
    steps = l // tl
    k = tl // t
    pitch = _relayout_pitch(e)
    return pl.pallas_call(
        functools.partial(_s5_pre_kernel, e=e, t=t, sub=min(S5_SUB, tl), pitch=pitch),
        grid=(b, steps),
        in_specs=[pl.BlockSpec((1, tl, d), lambda i, j: (i, j, 0)),
                  pl.BlockSpec((1, 1, d), lambda i, j: (i, 0, 0)),
                  pl.BlockSpec((1, 1, d), lambda i, j: (i, 0, 0)),
                  pl.BlockSpec((1, d), lambda i, j: (0, 0)),
                  pl.BlockSpec((2 * e, d), lambda i, j: (0, 0))],
        out_specs=[pl.BlockSpec((e, k, t), lambda i, j: (0, i * steps + j, 0)),
                   pl.BlockSpec((1, 2 * e, tl), lambda i, j: (i, 0, j))],
        out_shape=[_sds((e, b * l // t, t), F32), _sds((b, 2 * e, l), BF16)],
        scratch_shapes=[pltpu.VMEM((k * pitch, t), F32)],
        compiler_params=_params("arbitrary", "arbitrary"),
        name="s5_pre",
    )(x, shift, scale, g, w_in_t)


def _abar(lre, lim, ldt):
    dt = jnp.exp(ldt)
    mag = jnp.exp(lre * dt)
    are = mag * jnp.cos(lim * dt)
    aim = mag * jnp.sin(lim * dt)
    den = lre * lre + lim * lim
    zre = ((are - 1.0) * lre + aim * lim) / den
    zim = (aim * lre - (are - 1.0) * lim) / den
    return are, aim, zre, zim


def _cplx_pow(ar, ai, expo, nbits):
    shape = expo.shape
    rr = jnp.ones(shape, F32)
    ri = jnp.zeros(shape, F32)
    pr, pi = ar, ai
    for k in range(nbits):
        bit = ((expo >> k) & 1) == 1
        mr = jnp.where(bit, pr, 1.0)
        mi = jnp.where(bit, pi, 0.0)
        rr, ri = rr * mr - ri * mi, rr * mi + ri * mr
        if k + 1 < nbits:
            pr, pi = pr * pr - pi * pi, 2.0 * pr * pi
    return rr, ri


def _s5_tables(prm, xmat_scr, ymat_scr, adec_scr, kw_scr, *, t, n, p):
    lre, lim, ldt, btre, btim, cre, cim = prm
    n2 = 2 * n
    nbits = (t - 1).bit_length()
    lane_row = lax.broadcasted_iota(jnp.int32, (1, n2), 1)
    lane_tn = lax.broadcasted_iota(jnp.int32, (t, n2), 1)
    sub_tn = lax.broadcasted_iota(jnp.int32, (t, n2), 0)
    lane_p = lax.broadcasted_iota(jnp.int32, (p, n2), 1)
    lane_pp = lax.broadcasted_iota(jnp.int32, (p * p, n2), 1)
    first_row = lane_row < n
    first_tn = lane_tn < n

    k_halves = []
    k0_bwd = None
    for d in range(2):
        are, aim, zre, zim = _abar(lre[d], lim[d], ldt[d])

        pr, pi = are, aim
        for _ in range(t.bit_length() - 1):
            pr, pi = pr * pr - pi * pi, 2.0 * pr * pi
        adec_scr[2 * d:2 * d + 1, :] = pr
        adec_scr[2 * d + 1:2 * d + 2, :] = jnp.where(first_row, -pi, pi)

        bbt_re = zre * btre[d] - zim * btim[d]
        bbt_im = zre * btim[d] + zim * btre[d]

        asc_re, asc_im = _cplx_pow(are, aim, sub_tn, nbits)
        dsc_re, dsc_im = _cplx_pow(are, aim, t - 1 - sub_tn, nbits)
        up_re, up_im = (asc_re, asc_im) if d == 0 else (dsc_re, dsc_im)
        up_re, up_im = up_re * are - up_im * aim, up_re * aim + up_im * are

        rows = []
        for pp in range(p):
            c_re = cre[d][pp:pp + 1, :]
            c_im = cim[d][pp:pp + 1, :]
            cb_re = c_re * bbt_re - c_im * bbt_im
            cb_im = c_re * bbt_im + c_im * bbt_re
            rows.append(jnp.where(lane_p < n, cb_re, -cb_im))
        cbw = jnp.concatenate(rows, axis=0)
        lag_re, lag_im = (asc_re, asc_im) if d == 0 else (up_re, up_im)
        k_halves.append(_nt_dot(cbw, jnp.where(first_tn, lag_re, lag_im), precision=HIGHEST))
        if d == 1:
            k0_bwd = jnp.sum(jnp.where(lane_pp < n, cbw, 0.0), axis=1, keepdims=True)

        xt_re, xt_im = (dsc_re, dsc_im) if d == 0 else (asc_re, asc_im)
        for q in range(p):
            b_re = bbt_re[q:q + 1, :]
            b_im = bbt_im[q:q + 1, :]
            v1 = jnp.where(first_row, b_re, b_im)
            v2 = jnp.where(first_row, b_im, -b_re)
            xmat_scr[q * t:(q + 1) * t, d * n2:(d + 1) * n2] = (xt_re * v1 - xt_im * v2).astype(BF16)

        ya = jnp.where(first_tn, up_re, -up_im)
        yb = jnp.where(first_tn, up_im, up_re)
        for pp in range(p):
            ymat_scr[pp * t:(pp + 1) * t, d * n2:(d + 1) * n2] = (
                cre[d][pp:pp + 1, :] * ya - cim[d][pp:pp + 1, :] * yb).astype(BF16)

    lane_k = lax.broadcasted_iota(jnp.int32, (p * p, t), 1)
    kfull = jnp.concatenate([k_halves[0] + jnp.where(lane_k == 0, k0_bwd, 0.0), k_halves[1]], axis=1)
    kprev = pltpu.roll(kfull, 1, 1)
    lo = pltpu.bitcast(kfull.astype(BF16).astype(F32), U32) >> 16
    hi = pltpu.bitcast(kprev.astype(BF16).astype(F32), U32) & jnp.uint32(0xFFFF0000)
    words = hi | lo
    kw_scr[0] = words[:, :t]
    kw_scr[1] = words[:, t:]


def _s5_wrap_mask(t):
    wr = lax.broadcasted_iota(jnp.int32, (t // 2, t), 0)
    ln = lax.broadcasted_iota(jnp.int32, (t // 2, t), 1)
    return ln + 2 * wr <= t - 1


def _s5_toep_tile(kw_scr, toep_scr, q, pp, wrap_mask, *, t, p):
    r = pp * p + q
    full = jnp.where(wrap_mask, kw_scr[0, pl.ds(r, 1), :], kw_scr[1, pl.ds(r, 1), :])
    rolled = pltpu.roll(full, 0, 1, stride=2, stride_axis=0)
    start = q * t if isinstance(q, int) else pl.multiple_of(q * t, t)
    toep_scr[pl.ds(start, t), pp * t:(pp + 1) * t] = pltpu.bitcast(rolled, BF16)


def _s5_ssm_kernel(u_ref, *refs, nc, p, t, n):
    cur_prm, nxt_prm, y_ref = refs[0:7], refs[7:14], refs[14]
    toep_scr, xmat_scr, ymat_scr, adec_scr, kw_scr, lhs_scr, xs_scr, xw_scr, sin_scr = refs[15:]
    n2 = 2 * n
    bsz = u_ref.shape[1] // nc
    g = pl.program_id(0)
    slot = lax.rem(g, 2)
    nslot = 1 - slot
    kw = dict(t=t, n=n, p=p)
    wrap_mask = _s5_wrap_mask(t)

    @pl.when(g == 0)
    def _():
        _s5_tables(cur_prm, xmat_scr.at[0], ymat_scr.at[0], adec_scr.at[0], kw_scr.at[0], **kw)

        def q_body(q, carry):
            for pp in range(p):
                _s5_toep_tile(kw_scr.at[0], toep_scr.at[0], q, pp, wrap_mask, t=t, p=p)
            return carry

        lax.fori_loop(0, p, q_body, 0)

    _s5_tables(nxt_prm, xmat_scr.at[nslot], ymat_scr.at[nslot], adec_scr.at[nslot], kw_scr.at[nslot], **kw)

    for q in range(p):
        lhs_scr[:, q * t:(q + 1) * t] = u_ref[q].astype(BF16)
    lhs = lhs_scr[...]

    xmat = xmat_scr.at[slot]
    ymat = ymat_scr.at[slot]
    adec = adec_scr.at[slot]
    toep = toep_scr.at[slot]
    xs_both = jnp.dot(lhs, xmat[...], preferred_element_type=F32)
    for d in range(2):
        xs = xs_both[:, d * n2:(d + 1) * n2]
        xs_scr[d] = xs
        xw_scr[d] = pltpu.roll(xs, n, 1)

    for d in range(2):
        a1 = adec[2 * d:2 * d + 1, :]
        a2 = adec[2 * d + 1:2 * d + 2, :]
        s = jnp.zeros((bsz, n2), F32)
        sw = jnp.zeros((bsz, n2), F32)
        for c in (range(nc) if d == 0 else reversed(range(nc))):
            rows = pl.ds(c, bsz, stride=nc)
            sin_scr[d, rows, :] = s
            s, sw = (a1 * s + a2 * sw + xs_scr[d, rows, :],
                     a1 * sw - a2 * s + xw_scr[d, rows, :])

    sin_both = jnp.concatenate([sin_scr[0].astype(BF16), sin_scr[1].astype(BF16)], axis=1)
    w = V7X_MXU_DIM
    npanels = p * t // w
    tiles = [(q, pp) for q in range(p) for pp in range(p)]
    per_panel = len(tiles) // npanels
    for pn in range(npanels):
        cols = slice(pn * w, (pn + 1) * w)
        acc = jnp.dot(lhs, toep[:, cols], preferred_element_type=F32)
        acc = acc + _nt_dot(sin_both, ymat[cols, :])
        for h in range(w // t):
            y_ref[pn * (w // t) + h] = acc[:, h * t:(h + 1) * t]
        for q, pp in tiles[pn * per_panel:(pn + 1) * per_panel]:
            _s5_toep_tile(kw_scr.at[nslot], toep_scr.at[nslot], q, pp, wrap_mask, t=t, p=p)


def _s5_ssm(ut, nc, lam_re, lam_im, log_dt, b_re, b_im, c_re, c_im):
    e, bc, t = ut.shape
    _, g, n, p = b_re.shape
    n2 = 2 * n
    pt = p * t
    dup = lambda a: jnp.concatenate([a, a], axis=-1)
    lre = dup(lam_re)[:, :, None, :]
    lim = dup(lam_im)[:, :, None, :]
    ldt = log_dt[:, :, None, None]
    btre = dup(jnp.swapaxes(b_re, -1, -2))
    btim = dup(jnp.swapaxes(b_im, -1, -2))
    cre = dup(c_re)
    cim = dup(c_im)
    prm = (lre, lim, ldt, btre, btim, cre, cim)

    def this_group(a):
        return pl.BlockSpec((2, None) + a.shape[2:], lambda i: (0, i, 0, 0))

    def next_group(a):
        return pl.BlockSpec((2, None) + a.shape[2:], lambda i: (0, jnp.minimum(i + 1, g - 1), 0, 0))

    return pl.pallas_call(
        functools.partial(_s5_ssm_kernel, nc=nc, p=p, t=t, n=n),
        grid=(g,),
        in_specs=([pl.BlockSpec((p, bc, t), lambda i: (i, 0, 0))]
                  + [this_group(a) for a in prm] + [next_group(a) for a in prm]),
        out_specs=pl.BlockSpec((p, bc, t), lambda i: (i, 0, 0)),
        out_shape=_sds((e, bc, t), F32),
        scratch_shapes=[pltpu.VMEM((2, pt, pt), BF16),
                        pltpu.VMEM((2, pt, 2 * n2), BF16),
                        pltpu.VMEM((2, pt, 2 * n2), BF16),
                        pltpu.VMEM((2, 4, n2), F32),
                        pltpu.VMEM((2, 2, p * p, t), U32),
                        pltpu.VMEM((bc, pt), BF16),
                        pltpu.VMEM((2, bc, n2), F32),
                        pltpu.VMEM((2, bc, n2), F32),
                        pltpu.VMEM((2, bc, n2), F32)],
        compiler_params=_params("arbitrary"),
        name="s5_ssm",
    )(ut, *prm, *prm)


def _s5_post_kernel(y_ref, uzt_ref, x_ref, gate_ref, dcol_ref, bcol_ref, wgt_ref, wot_ref,
                    o_ref, rl_scr, *, e, t, sub, pitch):
    for ch in range(e):
        rl_scr[pl.ds(ch, V7X_SUBLANES, stride=pitch), :] = y_ref[ch]

    k = sub // t
    reps = sub // dcol_ref.shape[1]
    dfull = _cat([dcol_ref[...]] * reps, 1)
    bfull = _cat([bcol_ref[...]] * reps, 1)
    for s in range(x_ref.shape[1] // sub):
        tok = slice(s * sub, (s + 1) * sub)
        yt = _cat([rl_scr[(s * k + kk) * pitch:(s * k + kk) * pitch + e, :] for kk in range(k)], 1)
        ut = uzt_ref[0, :e, tok].astype(F32)
        y = jax.nn.gelu(yt + dfull * ut)
        g = jnp.dot(wgt_ref[...], y.astype(BF16), preferred_element_type=F32) + bfull
        y = y * _sigmoid(g)
        z = uzt_ref[0, e:, tok].astype(F32)
        yz = y * _silu(z)
        ot = jnp.dot(wot_ref[...], yz.astype(BF16), preferred_element_type=F32)
        o_ref[0, tok, :] = x_ref[0, tok, :] + gate_ref[0] * ot.T


def _s5_post(yt, uzt, x, gate, d_skip, b_glu, w_glu_t, w_out_t):
    b, l, d = x.shape
    e, bc, t = yt.shape
    tl = S5_TILE
    k = tl // t
    lanes = V7X_LANES
    dcol = jnp.broadcast_to(d_skip[:, None], (e, lanes))
    bcol = jnp.broadcast_to(b_glu[:, None], (e, lanes))
    steps = l // tl
    pitch = _relayout_pitch(e)
    return pl.pallas_call(
        functools.partial(_s5_post_kernel, e=e, t=t, sub=tl, pitch=pitch),
        grid=(b, steps),
        in_specs=[pl.BlockSpec((e, k, t), lambda i, j: (0, i * steps + j, 0)),
                  pl.BlockSpec((1, 2 * e, tl), lambda i, j: (i, 0, j)),
                  pl.BlockSpec((1, tl, d), lambda i, j: (i, j, 0)),
                  pl.BlockSpec((1, 1, d), lambda i, j: (i, 0, 0)),
                  pl.BlockSpec((e, lanes), lambda i, j: (0, 0)),
                  pl.BlockSpec((e, lanes), lambda i, j: (0, 0)),
                  pl.BlockSpec((e, e), lambda i, j: (0, 0)),
                  pl.BlockSpec((d, e), lambda i, j: (0, 0))],
        out_specs=pl.BlockSpec((1, tl, d), lambda i, j: (i, j, 0)),
        out_shape=_sds((b, l, d), F32),
        scratch_shapes=[pltpu.VMEM((k * pitch, t), F32)],
        compiler_params=_params("arbitrary", "arbitrary"),
        name="s5_post",
    )(yt, uzt, x, gate, dcol, bcol, w_glu_t, w_out_t)


def _s5_layer(x, shift, scale, gate, g, w_in, lam_re, lam_im, log_dt, b_re, b_im, c_re, c_im,
              d_skip, w_glu, b_glu, w_out):
    l = x.shape[1]
    assert l % S5_TILE == 0
    ut, uzt = _s5_pre(x, shift, scale, g, w_in.T.astype(BF16))
    yt = _s5_ssm(ut, l // S5_CHUNK, lam_re, lam_im, log_dt, b_re, b_im, c_re, c_im)
    return _s5_post(yt, uzt, x, gate, d_skip, b_glu, w_glu.T.astype(BF16), w_out.T.astype(BF16))


def _ml_pre_kernel(x_ref, shift_ref, scale_ref, g_ref, w_ref, xm_ref, z_ref, *, e):
    h = _modulated_norm(x_ref[0], g_ref[...], shift_ref[0], scale_ref[0])
    xz = jnp.dot(h.astype(BF16), w_ref[...], preferred_element_type=F32)
    xm_ref[0] = xz[:, :e].astype(BF16)
    z_ref[0] = xz[:, e:].astype(BF16)


def _ml_pre(x, shift, scale, g, w_in):
    b, l, d = x.shape
    e = w_in.shape[1] // 2
    tl = min(TOKEN_TILE, l)
    return pl.pallas_call(
        functools.partial(_ml_pre_kernel, e=e),
        grid=(b, l // tl),
        in_specs=[pl.BlockSpec((1, tl, d), lambda i, j: (i, j, 0)),
                  pl.BlockSpec((1, 1, d), lambda i, j: (i, 0, 0)),
                  pl.BlockSpec((1, 1, d), lambda i, j: (i, 0, 0)),
                  pl.BlockSpec((1, d), lambda i, j: (0, 0)),
                  pl.BlockSpec((d, 2 * e), lambda i, j: (0, 0))],
        out_specs=[pl.BlockSpec((1, tl, e), lambda i, j: (i, j, 0)),
                   pl.BlockSpec((1, tl, e), lambda i, j: (i, j, 0))],
        out_shape=[_sds((b, l, e), BF16), _sds((b, l, e), BF16)],
        compiler_params=_params("arbitrary", "arbitrary"),
        name="ml_pre",
    )(x, shift, scale, g, w_in)


HALO = 2 * V7X_SUBLANES


def _ml_qkv_kernel(xm_ref, prev_ref, next_ref, cw_ref, cb_ref, wq_ref, wk_ref, wv_ref, wg_ref,
                   bg_ref, q_ref, kt_ref, v_ref, xc_ref, gp_ref, *, steps, taps):
    j = pl.program_id(1)
    xm_b = xm_ref[0]
    tq, e = xm_b.shape
    zero = jnp.zeros((), BF16)
    prev = jnp.where(j == 0, zero, prev_ref[0])
    nxt = jnp.where(j == steps - 1, zero, next_ref[0])
    xe = jnp.concatenate([prev, xm_b, nxt], axis=0).astype(F32)
    pad = taps // 2
    acc = jnp.broadcast_to(cb_ref[...], (tq, e))
    for kk in range(taps):
        off = HALO + kk - pad
        acc = acc + xe[off:off + tq, :] * cw_ref[kk:kk + 1, :]
    xc = _silu(acc)
    xc_b = xc.astype(BF16)
    tile = wq_ref.shape[-1]
    qs, ks, vs = [], [], []
    for i in range(e // tile):
        cols = slice(i * tile, (i + 1) * tile)
        qs.append(jnp.dot(xc_b[:, cols], wq_ref[i], preferred_element_type=F32))
        ks.append(jnp.dot(xc_b[:, cols], wk_ref[i], preferred_element_type=F32).astype(BF16))
        vs.append(jnp.dot(xm_b[:, cols], wv_ref[i], preferred_element_type=F32))
    q = jnp.concatenate(qs, axis=1).astype(BF16)
    k = jnp.concatenate(ks, axis=1)
    v = jnp.concatenate(vs, axis=1).astype(BF16)
    gp = (jnp.dot(q, wg_ref[0], preferred_element_type=F32)
          + jnp.dot(k, wg_ref[1], preferred_element_type=F32)
          + jnp.dot(v, wg_ref[2], preferred_element_type=F32) + bg_ref[...])
    q_ref[0] = q
    for i in range(e // tile):
        kt_ref[0, i * tile:(i + 1) * tile, :] = ks[i].T
    v_ref[0] = v
    xc_ref[0] = xc_b
    gp_ref[0] = gp


def _blockdiag_tiles(w, tile):
    nblk, bs, _ = w.shape
    rows = w.reshape(nblk // (tile // bs), tile, bs)
    spread = jnp.tile(rows, (1, 1, tile // bs))
    blk = jnp.arange(tile) // bs
    return jnp.where(blk[:, None] == blk[None, :], spread, 0.0)


def _ml_qkv(xm, conv_w, conv_b, w_q, w_k, w_v, w_gates, b_gates):
    b, l, e = xm.shape
    tq = min(TOKEN_TILE, l)
    steps = l // tq
    taps = conv_w.shape[0]
    tile = V7X_MXU_DIM
    lanes = V7X_LANES
    ng = w_gates.shape[-1]
    wq = _blockdiag_tiles(w_q, tile).astype(BF16)
    wk = _blockdiag_tiles(w_k, tile).astype(BF16)
    wv = _blockdiag_tiles(w_v, tile).astype(BF16)
    wg = jnp.pad(w_gates, ((0, 0), (0, 0), (0, lanes - ng))).astype(BF16)
    bg = jnp.pad(b_gates, (0, lanes - ng)).reshape(1, lanes)
    hb = tq // HALO
    last = l // HALO - 1
    tok = lambda i, j: (i, j, 0)
    full3 = lambda i, j: (0, 0, 0)
    full2 = lambda i, j: (0, 0)
    return pl.pallas_call(
        functools.partial(_ml_qkv_kernel, steps=steps, taps=taps),
        grid=(b, steps),
        in_specs=[pl.BlockSpec((1, tq, e), tok),
                  pl.BlockSpec((1, HALO, e), lambda i, j: (i, jnp.maximum(j * hb - 1, 0), 0)),
                  pl.BlockSpec((1, HALO, e), lambda i, j: (i, jnp.minimum((j + 1) * hb, last), 0)),
                  pl.BlockSpec((taps, e), full2),
                  pl.BlockSpec((1, e), full2),
                  pl.BlockSpec(wq.shape, full3),
                  pl.BlockSpec(wk.shape, full3),
                  pl.BlockSpec(wv.shape, full3),
                  pl.BlockSpec(wg.shape, full3),
                  pl.BlockSpec((1, lanes), full2)],
        out_specs=[pl.BlockSpec((1, tq, e), tok), pl.BlockSpec((1, e, tq), lambda i, j: (i, 0, j)),
                   pl.BlockSpec((1, tq, e), tok), pl.BlockSpec((1, tq, e), tok),
                   pl.BlockSpec((1, tq, lanes), tok)],
        out_shape=[_sds((b, l, e), BF16), _sds((b, e, l), BF16), _sds((b, l, e), BF16),
                   _sds((b, l, e), BF16), _sds((b, l, lanes), F32)],
        compiler_params=_params("arbitrary", "arbitrary"),
        name="ml_qkv",
    )(xm, xm, xm, conv_w, conv_b.reshape(1, e), wq, wk, wv, wg, bg)


def _ml_gates_kernel(gp_ref, cb_ref, gg_ref, at_ref, gt_ref, *, nf, t):
    lanes = gp_ref.shape[2]
    ri = lax.broadcasted_iota(jnp.int32, (t, t), 0)
    ci = lax.broadcasted_iota(jnp.int32, (t, t), 1)
    lower = jnp.where(ci <= ri, 1.0, 0.0).astype(BF16)
    upper = jnp.where(ci >= ri, 1.0, 0.0).astype(BF16)

    def tri_sums(tri, parts):
        return sum(jnp.dot(tri, part, preferred_element_type=F32) for part in parts)

    fwd = lax.broadcasted_iota(jnp.int32, (t, lanes), 1) < 2 * nf
    fwd_row = lax.broadcasted_iota(jnp.int32, (1, lanes), 1) < 2 * nf
    for c in range(gp_ref.shape[1] // t):
        rows = slice(c * t, (c + 1) * t)
        x = gp_ref[0, rows, :]
        lf = jax.nn.log_sigmoid(x)
        ipre = pltpu.roll(x, nf, 1)
        hi = lf.astype(BF16)
        rest = lf - hi.astype(F32)
        mid = rest.astype(BF16)
        low = (rest - mid.astype(F32)).astype(BF16)
        cum_f = tri_sums(lower, (hi, mid, low))
        cum_b = tri_sums(upper, (hi, mid, low))
        cum = jnp.where(fwd, cum_f, cum_b)
        tot = jnp.where(fwd_row, cum_f[t - 1:t, :], cum_b[0:1, :])
        gg = tot - cum + ipre
        cb_ref[0, rows, :] = cum
        gg_ref[0, rows, :] = gg
        at_ref[0, c] = (ipre - cum).T
        gt_ref[0, c] = gg.T


def _ml_gates(gp, nheads):
    b, l, lanes = gp.shape
    t = ML_CHUNK
    nc = l // t
    return pl.pallas_call(
        functools.partial(_ml_gates_kernel, nf=nheads, t=t),
        grid=(b,),
        in_specs=[pl.BlockSpec((1, l, lanes), lambda i: (i, 0, 0))],
        out_specs=[pl.BlockSpec((1, l, lanes), lambda i: (i, 0, 0)),
                   pl.BlockSpec((1, l, lanes), lambda i: (i, 0, 0)),
                   pl.BlockSpec((1, nc, lanes, t), lambda i: (i, 0, 0, 0)),
                   pl.BlockSpec((1, nc, lanes, t), lambda i: (i, 0, 0, 0))],
        out_shape=[_sds((b, l, lanes), F32), _sds((b, l, lanes), F32), _sds((b, nc, lanes, t), F32),
                   _sds((b, nc, lanes, t), F32)],
        compiler_params=_params("arbitrary"),
        name="ml_gates",
    )(gp)


def _ml_main_kernel(q_ref, kt_ref, v_ref, cb_ref, gg_ref, at_ref, gt_ref, xc_ref, z_ref, x_ref, gate_ref,
                    gnw_ref, skip_ref, wout_ref, *rest, nc, nh, final):
    if final:
        fg_ref, o_ref, c_scr, n_scr, m_scr, hf_scr = rest
    else:
        o_ref, c_scr, n_scr, m_scr, hf_scr = rest
    dirn = pl.program_id(1)
    step = pl.program_id(2)
    t = q_ref.shape[1]
    dh = q_ref.shape[2] // nh
    scale = float(dh) ** -0.5

    @pl.when(step == 0)
    def _():
        c_scr[...] = jnp.zeros_like(c_scr)
        n_scr[...] = jnp.zeros_like(n_scr)
        m_scr[...] = jnp.zeros_like(m_scr)

    ri = lax.broadcasted_iota(jnp.int32, (t, t), 0)
    ci = lax.broadcasted_iota(jnp.int32, (t, t), 1)
    ones_rows = jnp.ones((V7X_SUBLANES, t), BF16)

    def heads_out(d):
        keep = (ci <= ri) if d == 0 else (ci >= ri)
        stage = []
        for h in range(nh):
            cols = slice(h * dh, (h + 1) * dh)
            gcol = d * 2 * nh + nh + h
            q = q_ref[0, :, cols]
            kt = kt_ref[0, cols, :]
            cb = cb_ref[0, :, gcol:gcol + 1]
            gg = gg_ref[0, :, gcol:gcol + 1]
            a_row = at_ref[0, 0, gcol:gcol + 1, :]
            g_row = gt_ref[0, 0, gcol:gcol + 1, :]
            tot = cb[t - 1:t, :] if d == 0 else cb[0:1, :]
            m = m_scr[h][:, 0:1]
            dmat = jnp.where(keep, cb + a_row, -jnp.inf)
            inter = cb + m
            m_t = jnp.maximum(inter, jnp.max(dmat, axis=1, keepdims=True))
            m_new = jnp.maximum(tot + m, jnp.max(gg, axis=0, keepdims=True))
            stage.append(dict(
                q=q, kt=kt, m_t=m_t, m_new=m_new,
                w_intra=jnp.exp(dmat - m_t),
                w_inter=jnp.exp(inter - m_t),
                qk=jnp.dot(q, kt, preferred_element_type=F32),
                decay=jnp.exp(tot + m - m_new),
                wk_row=jnp.exp(g_row - m_new) * scale))
        for h in range(nh):
            cols = slice(h * dh, (h + 1) * dh)
            a = stage[h]
            v = v_ref[0, :, cols]
            s = a['qk'] * scale * a['w_intra']
            qw = a['q'] * a['w_inter'].astype(BF16)
            num = jnp.dot(s.astype(BF16), v, preferred_element_type=F32)
            a['num'] = num + jnp.dot(qw, c_scr[h].astype(BF16), preferred_element_type=F32)
            qn = jnp.sum(a['q'].astype(F32) * n_scr[h], axis=1, keepdims=True)
            a['nq'] = jnp.sum(s, axis=1, keepdims=True) + a['w_inter'] * qn
        for h in range(nh):
            cols = slice(h * dh, (h + 1) * dh)
            a = stage[h]
            v = v_ref[0, :, cols]
            a['hout'] = a['num'] / jnp.maximum(jnp.abs(a['nq']), jnp.exp(-a['m_t']))
            kwt = a['kt'] * a['wk_row'].astype(BF16)
            a['c_new'] = a['decay'] * c_scr[h] + jnp.dot(kwt, v, preferred_element_type=F32)
            a['n_add'] = _nt_dot(ones_rows, kwt)[0:1, :]
        for h in range(nh):
            a = stage[h]
            c_scr[h] = a['c_new']
            n_scr[h] = a['decay'] * n_scr[h] + a['n_add']
            m_scr[h] = jnp.broadcast_to(a['m_new'], m_scr.shape[1:])
        return [a['hout'] for a in stage]

    @pl.when(dirn == 0)
    def _():
        rows = pl.ds(pl.multiple_of(step * t, t), t)
        for h, hout in enumerate(heads_out(0)):
            hf_scr[rows, h * dh:(h + 1) * dh] = hout

    @pl.when(dirn == 1)
    def _():
        rows = pl.ds(pl.multiple_of((nc - 1 - step) * t, t), t)
        acc = None
        for h, hout in enumerate(heads_out(1)):
            cols = slice(h * dh, (h + 1) * dh)
            hs = hf_scr[rows, cols] + hout
            mu = jnp.mean(hs, axis=1, keepdims=True)
            cen = hs - mu
            var = jnp.mean(cen * cen, axis=1, keepdims=True)
            hn = cen * lax.rsqrt(var + LN_EPS)
            y = hn * gnw_ref[:, cols] + skip_ref[:, cols] * xc_ref[0, :, cols].astype(F32)
            y = y * _silu(z_ref[0, :, cols].astype(F32))
            part = jnp.dot(y.astype(BF16), wout_ref[cols, :], preferred_element_type=F32)
            acc = part if acc is None else acc + part
        out = x_ref[0] + gate_ref[0] * acc
        o_ref[0] = _rms(out, fg_ref[...]) if final else out


def _ml_main(q, kt, v, cb, gg, at, gt, xc, z, x, gate, gn_w, skip, w_out, nheads, final_g):
    b, l, e = q.shape
    d = x.shape[-1]
    t = ML_CHUNK
    nc = l // t
    lanes = cb.shape[-1]
    dh = e // nheads
    chunk = lambda i, dd, s: (i, s + dd * (nc - 1 - 2 * s), 0)
    chunk_t = lambda i, dd, s: (i, 0, s + dd * (nc - 1 - 2 * s))
    chunk4 = lambda i, dd, s: (i, s + dd * (nc - 1 - 2 * s), 0, 0)
    late = lambda i, dd, s: (i, (nc - 1) - dd * s, 0)
    vec = lambda i, dd, s: (0, 0)
    final = final_g is not None
    extra_specs = [pl.BlockSpec((1, d), vec)] if final else []
    extra_args = [final_g.reshape(1, d)] if final else []
    return pl.pallas_call(
        functools.partial(_ml_main_kernel, nc=nc, nh=nheads, final=final),
        grid=(b, 2, nc),
        in_specs=[pl.BlockSpec((1, t, e), chunk),
                  pl.BlockSpec((1, e, t), chunk_t),
                  pl.BlockSpec((1, t, e), chunk),
                  pl.BlockSpec((1, t, lanes), chunk),
                  pl.BlockSpec((1, t, lanes), chunk),
                  pl.BlockSpec((1, 1, lanes, t), chunk4),
                  pl.BlockSpec((1, 1, lanes, t), chunk4),
                  pl.BlockSpec((1, t, e), late),
                  pl.BlockSpec((1, t, e), late),
                  pl.BlockSpec((1, t, d), late),
                  pl.BlockSpec((1, 1, d), lambda i, dd, s: (i, 0, 0)),
                  pl.BlockSpec((1, e), vec),
                  pl.BlockSpec((1, e), vec),
                  pl.BlockSpec((e, d), vec)] + extra_specs,
        out_specs=pl.BlockSpec((1, t, d), late),
        out_shape=_sds((b, l, d), F32),
        scratch_shapes=[pltpu.VMEM((nheads, dh, dh), F32),
                        pltpu.VMEM((nheads, 1, dh), F32),
                        pltpu.VMEM((nheads, 1, V7X_LANES), F32),
                        pltpu.VMEM((l, e), F32)],
        compiler_params=_params("arbitrary", "arbitrary", "arbitrary"),
        name="ml_main",
    )(q, kt, v, cb, gg, at, gt, xc, z, x, gate, gn_w.reshape(1, e), skip.reshape(1, e), w_out.astype(BF16),
      *extra_args)


def _ml_layer(x, shift, scale, gate, g, w_in, conv_w, conv_b, w_q, w_k, w_v, w_gates, b_gates,
              gn_w, skip, w_out, final_g):
    assert x.shape[1] % ML_CHUNK == 0
    nheads = b_gates.shape[0] // 4
    xm, z = _ml_pre(x, shift, scale, g, w_in.astype(BF16))
    q, kt, v, xc, gp = _ml_qkv(xm, conv_w, conv_b, w_q, w_k, w_v, w_gates, b_gates)
    cb, gg, at, gt = _ml_gates(gp, nheads)
    return _ml_main(q, kt, v, cb, gg, at, gt, xc, z, x, gate, gn_w, skip, w_out, nheads, final_g)


def _final_norm_kernel(x_ref, g_ref, o_ref):
    o_ref[0] = _rms(x_ref[0], g_ref[...])


def _final_norm(x, g):
    b, l, d = x.shape
    tl = min(TOKEN_TILE, l)
    return pl.pallas_call(
        _final_norm_kernel,
        grid=(b, l // tl),
        in_specs=[pl.BlockSpec((1, tl, d), lambda i, j: (i, j, 0)),
                  pl.BlockSpec((1, d), lambda i, j: (0, 0))],
        out_specs=pl.BlockSpec((1, tl, d), lambda i, j: (i, j, 0)),
        out_shape=_sds((b, l, d), F32),
        compiler_params=_params("arbitrary", "arbitrary"),
        name="final_norm",
    )(x, g.reshape(1, d))


def kernel(x, c, ada_w, ada_b, norm_g, s5_w_in, s5_lam_re, s5_lam_im, s5_log_dt, s5_b_re, s5_b_im, s5_c_re, s5_c_im, s5_d, s5_w_glu, s5_b_glu, s5_w_out, ml_w_in, ml_conv_w, ml_conv_b, ml_w_q, ml_w_k, ml_w_v, ml_w_gates, ml_b_gates, ml_gn_w, ml_skip, ml_w_out, final_g):
    depth = ada_w.shape[0]
    d = x.shape[-1]
    mod = _ada(c, ada_w, ada_b)
    for i in range(depth):
        shift = mod[i, :, None, 0:d]
        scale = mod[i, :, None, d:2 * d]
        gate = mod[i, :, None, 2 * d:3 * d]
        g = norm_g[i].reshape(1, d)
        j = i // 2
        if i % 2 == 0:
            x = _s5_layer(x, shift, scale, gate, g, s5_w_in[j], s5_lam_re[j], s5_lam_im[j], s5_log_dt[j],
                          s5_b_re[j], s5_b_im[j], s5_c_re[j], s5_c_im[j], s5_d[j], s5_w_glu[j],
                          s5_b_glu[j], s5_w_out[j])
        else:
            x = _ml_layer(x, shift, scale, gate, g, ml_w_in[j], ml_conv_w[j], ml_conv_b[j], ml_w_q[j],
                          ml_w_k[j], ml_w_v[j], ml_w_gates[j], ml_b_gates[j], ml_gn_w[j], ml_skip[j],
                          ml_w_out[j], final_g if i == depth - 1 else None)
    return x if depth % 2 == 0 else _final_norm(x, final_g)
```

```python
import functools

import jax
import jax.numpy as jnp
from jax import lax
from jax.experimental import pallas as pl
from jax.experimental.pallas import tpu as pltpu

F32 = jnp.float32
BF16 = jnp.bfloat16
U32 = jnp.uint32
HIGHEST = lax.Precision.HIGHEST

RMS_EPS = 1e-6
LN_EPS = 1e-5
V7X_LANES = 128
V7X_SUBLANES = 8
V7X_MXU_DIM = 256
VMEM_LIMIT_BYTES = 56 * 1024 * 1024
S5_CHUNK = V7X_LANES
S5_TILE = V7X_SUBLANES * S5_CHUNK
S5_SUB = 512
ML_CHUNK = V7X_MXU_DIM
TOKEN_TILE = 512


def _params(*sem):
    return pltpu.CompilerParams(dimension_semantics=sem, vmem_limit_bytes=VMEM_LIMIT_BYTES)


def _sds(shape, dtype):
    return jax.ShapeDtypeStruct(shape, dtype)


def _sigmoid(v):
    return 0.5 * jnp.tanh(0.5 * v) + 0.5


def _silu(v):
    return v * _sigmoid(v)


def _nt_dot(a, b, precision=None):
    return lax.dot_general(a, b, (((1,), (1,)), ((), ())), preferred_element_type=F32,
                           precision=precision)


def _cat(parts, axis):
    return parts[0] if len(parts) == 1 else jnp.concatenate(parts, axis=axis)


def _rms(x, g):
    ms = jnp.mean(x * x, axis=-1, keepdims=True)
    return x * lax.rsqrt(ms + RMS_EPS) * g


def _modulated_norm(x, g, shift, scale):
    return _rms(x, g) * (1.0 + scale) + shift


def _ada_kernel(c_ref, w_ref, b_ref, o_ref):
    sc = _silu(c_ref[...])
    o_ref[0] = jnp.dot(sc, w_ref[0], preferred_element_type=F32, precision=HIGHEST) + b_ref[0]


def _ada(c, ada_w, ada_b):
    depth, d, d3 = ada_w.shape
    b = c.shape[0]
    return pl.pallas_call(
        _ada_kernel,
        grid=(depth, d3 // d),
        in_specs=[pl.BlockSpec((b, d), lambda i, j: (0, 0)),
                  pl.BlockSpec((1, d, d), lambda i, j: (i, 0, j)),
                  pl.BlockSpec((1, 1, d), lambda i, j: (i, 0, j))],
        out_specs=pl.BlockSpec((1, b, d), lambda i, j: (i, 0, j)),
        out_shape=_sds((depth, b, d3), F32),
        compiler_params=_params("arbitrary", "arbitrary"),
        name="ada_mod",
    )(c, ada_w, ada_b.reshape(depth, 1, d3))


def _relayout_pitch(e):
    pitch = e + V7X_SUBLANES
    assert pitch % (2 * V7X_SUBLANES) == V7X_SUBLANES
    return pitch


def _s5_pre_kernel(x_ref, shift_ref, scale_ref, g_ref, w_ref, ut_ref, uzt_ref, rl_scr, *, e, t, sub, pitch):
    kk = sub // t
    for s in range(x_ref.shape[1] // sub):
        h = _modulated_norm(x_ref[0, s * sub:(s + 1) * sub, :], g_ref[...], shift_ref[0], scale_ref[0])
        uz = jnp.dot(h.astype(BF16), w_ref[...], preferred_element_type=F32).T
        uzt_ref[0, :, s * sub:(s + 1) * sub] = uz.astype(BF16)
        for k in range(kk):
            c = s * kk + k
            rl_scr[c * pitch:c * pitch + e, :] = uz[:e, k * t:(k + 1) * t]

    for ch in range(e):
        ut_ref[ch] = rl_scr[pl.ds(ch, V7X_SUBLANES, stride=pitch), :]


def _s5_pre(x, shift, scale, g, w_in):
    b, l, d = x.shape
    e = w_in.shape[1] // 2
    t, tl = S5_CHUNK, S5_TILE
    steps = l // tl
    k = tl // t
    pitch = _relayout_pitch(e)
    return pl.pallas_call(
        functools.partial(_s5_pre_kernel, e=e, t=t, sub=min(S5_SUB, tl), pitch=pitch),
        grid=(b, steps),
        in_specs=[pl.BlockSpec((1, tl, d), lambda i, j: (i, j, 0)),
                  pl.BlockSpec((1, 1, d), lambda i, j: (i, 0, 0)),
                  pl.BlockSpec((1, 1, d), lambda i, j: (i, 0, 0)),
                  pl.BlockSpec((1, d), lambda i, j: (0, 0)),
                  pl.BlockSpec((d, 2 * e), lambda i, j: (0, 0))],
        out_specs=[pl.BlockSpec((e, k, t), lambda i, j: (0, i * steps + j, 0)),
                   pl.BlockSpec((1, 2 * e, tl), lambda i, j: (i, 0, j))],
        out_shape=[_sds((e, b * l // t, t), F32), _sds((b, 2 * e, l), BF16)],
        scratch_shapes=[pltpu.VMEM((k * pitch, t), F32)],
        compiler_params=_params("arbitrary", "arbitrary"),
        name="s5_pre",
    )(x, shift, scale, g, w_in)


def _abar(lre, lim, ldt):
    dt = jnp.exp(ldt)
    mag = jnp.exp(lre * dt)
    are = mag * jnp.cos(lim * dt)
    aim = mag * jnp.sin(lim * dt)
    den = lre * lre + lim * lim
    zre = ((are - 1.0) * lre + aim * lim) / den
    zim = (aim * lre - (are - 1.0) * lim) / den
    return are, aim, zre, zim


def _cplx_pow(ar, ai, expo, nbits):
    shape = expo.shape
    rr = jnp.ones(shape, F32)
    ri = jnp.zeros(shape, F32)
    pr, pi = ar, ai
    for k in range(nbits):
        bit = ((expo >> k) & 1) == 1
        mr = jnp.where(bit, pr, 1.0)
        mi = jnp.where(bit, pi, 0.0)
        rr, ri = rr * mr - ri * mi, rr * mi + ri * mr
        if k + 1 < nbits:
            pr, pi = pr * pr - pi * pi, 2.0 * pr * pi
    return rr, ri


def _s5_tables(prm, xmat_scr, ymat_scr, adec_scr, kw_scr, *, t, n, p):
    lre, lim, ldt, btre, btim, cre, cim = prm
    n2 = 2 * n
    nbits = (t - 1).bit_length()
    lane_row = lax.broadcasted_iota(jnp.int32, (1, n2), 1)
    lane_tn = lax.broadcasted_iota(jnp.int32, (t, n2), 1)
    sub_tn = lax.broadcasted_iota(jnp.int32, (t, n2), 0)
    lane_p = lax.broadcasted_iota(jnp.int32, (p, n2), 1)
    lane_pp = lax.broadcasted_iota(jnp.int32, (p * p, n2), 1)
    first_row = lane_row < n
    first_tn = lane_tn < n

    k_halves = []
    k0_bwd = None
    for d in range(2):
        are, aim, zre, zim = _abar(lre[d], lim[d], ldt[d])

        pr, pi = are, aim
        for _ in range(t.bit_length() - 1):
            pr, pi = pr * pr - pi * pi, 2.0 * pr * pi
        adec_scr[2 * d:2 * d + 1, :] = pr
        adec_scr[2 * d + 1:2 * d + 2, :] = jnp.where(first_row, -pi, pi)

        bbt_re = zre * btre[d] - zim * btim[d]
        bbt_im = zre * btim[d] + zim * btre[d]

        asc_re, asc_im = _cplx_pow(are, aim, sub_tn, nbits)
        dsc_re, dsc_im = _cplx_pow(are, aim, t - 1 - sub_tn, nbits)
        up_re, up_im = (asc_re, asc_im) if d == 0 else (dsc_re, dsc_im)
        up_re, up_im = up_re * are - up_im * aim, up_re * aim + up_im * are

        rows = []
        for pp in range(p):
            c_re = cre[d][pp:pp + 1, :]
            c_im = cim[d][pp:pp + 1, :]
            cb_re = c_re * bbt_re - c_im * bbt_im
            cb_im = c_re * bbt_im + c_im * bbt_re
            rows.append(jnp.where(lane_p < n, cb_re, -cb_im))
        cbw = jnp.concatenate(rows, axis=0)
        lag_re, lag_im = (asc_re, asc_im) if d == 0 else (up_re, up_im)
        k_halves.append(_nt_dot(cbw, jnp.where(first_tn, lag_re, lag_im), precision=HIGHEST))
        if d == 1:
            k0_bwd = jnp.sum(jnp.where(lane_pp < n, cbw, 0.0), axis=1, keepdims=True)

        xt_re, xt_im = (dsc_re, dsc_im) if d == 0 else (asc_re, asc_im)
        for q in range(p):
            b_re = bbt_re[q:q + 1, :]
            b_im = bbt_im[q:q + 1, :]
            v1 = jnp.where(first_row, b_re, b_im)
            v2 = jnp.where(first_row, b_im, -b_re)
            xmat_scr[q * t:(q + 1) * t, d * n2:(d + 1) * n2] = (xt_re * v1 - xt_im * v2).astype(BF16)

        ya = jnp.where(first_tn, up_re, -up_im)
        yb = jnp.where(first_tn, up_im, up_re)
        for pp in range(p):
            ymat_scr[pp * t:(pp + 1) * t, d * n2:(d + 1) * n2] = (
                cre[d][pp:pp + 1, :] * ya - cim[d][pp:pp + 1, :] * yb).astype(BF16)

    lane_k = lax.broadcasted_iota(jnp.int32, (p * p, t), 1)
    kfull = jnp.concatenate([k_halves[0] + jnp.where(lane_k == 0, k0_bwd, 0.0), k_halves[1]], axis=1)
    kprev = pltpu.roll(kfull, 1, 1)
    lo = pltpu.bitcast(kfull.astype(BF16).astype(F32), U32) >> 16
    hi = pltpu.bitcast(kprev.astype(BF16).astype(F32), U32) & jnp.uint32(0xFFFF0000)
    words = hi | lo
    kw_scr[0] = words[:, :t]
    kw_scr[1] = words[:, t:]


def _s5_wrap_mask(t):
    wr = lax.broadcasted_iota(jnp.int32, (t // 2, t), 0)
    ln = lax.broadcasted_iota(jnp.int32, (t // 2, t), 1)
    return ln + 2 * wr <= t - 1


def _s5_toep_tile(kw_scr, toep_scr, q, pp, wrap_mask, *, t, p):
    r = pp * p + q
    full = jnp.where(wrap_mask, kw_scr[0, pl.ds(r, 1), :], kw_scr[1, pl.ds(r, 1), :])
    rolled = pltpu.roll(full, 0, 1, stride=2, stride_axis=0)
    start = q * t if isinstance(q, int) else pl.multiple_of(q * t, t)
    toep_scr[pl.ds(start, t), pp * t:(pp + 1) * t] = pltpu.bitcast(rolled, BF16)


def _s5_ssm_kernel(u_ref, *refs, nc, p, t, n):
    cur_prm, nxt_prm, y_ref = refs[0:7], refs[7:14], refs[14]
    toep_scr, xmat_scr, ymat_scr, adec_scr, kw_scr, lhs_scr, xs_scr, xw_scr, sin_scr = refs[15:]
    n2 = 2 * n
    bsz = u_ref.shape[1] // nc
    g = pl.program_id(0)
    slot = lax.rem(g, 2)
    nslot = 1 - slot
    kw = dict(t=t, n=n, p=p)
    wrap_mask = _s5_wrap_mask(t)

    @pl.when(g == 0)
    def _():
        _s5_tables(cur_prm, xmat_scr.at[0], ymat_scr.at[0], adec_scr.at[0], kw_scr.at[0], **kw)

        def q_body(q, carry):
            for pp in range(p):
                _s5_toep_tile(kw_scr.at[0], toep_scr.at[0], q, pp, wrap_mask, t=t, p=p)
            return carry

        lax.fori_loop(0, p, q_body, 0)

    _s5_tables(nxt_prm, xmat_scr.at[nslot], ymat_scr.at[nslot], adec_scr.at[nslot], kw_scr.at[nslot], **kw)

    for q in range(p):
        lhs_scr[:, q * t:(q + 1) * t] = u_ref[q].astype(BF16)
    lhs = lhs_scr[...]

    xmat = xmat_scr.at[slot]
    ymat = ymat_scr.at[slot]
    adec = adec_scr.at[slot]
    toep = toep_scr.at[slot]
    xs_both = jnp.dot(lhs, xmat[...], preferred_element_type=F32)
    for d in range(2):
        xs = xs_both[:, d * n2:(d + 1) * n2]
        xs_scr[d] = xs
        xw_scr[d] = pltpu.roll(xs, n, 1)

    for d in range(2):
        a1 = adec[2 * d:2 * d + 1, :]
        a2 = adec[2 * d + 1:2 * d + 2, :]
        s = jnp.zeros((bsz, n2), F32)
        sw = jnp.zeros((bsz, n2), F32)
        for c in (range(nc) if d == 0 else reversed(range(nc))):
            rows = pl.ds(c, bsz, stride=nc)
            sin_scr[d, rows, :] = s
            s, sw = (a1 * s + a2 * sw + xs_scr[d, rows, :],
                     a1 * sw - a2 * s + xw_scr[d, rows, :])

    sin_both = jnp.concatenate([sin_scr[0].astype(BF16), sin_scr[1].astype(BF16)], axis=1)
    w = V7X_MXU_DIM
    npanels = p * t // w
    tiles = [(q, pp) for q in range(p) for pp in range(p)]
    per_panel = len(tiles) // npanels
    for pn in range(npanels):
        cols = slice(pn * w, (pn + 1) * w)
        acc = jnp.dot(lhs, toep[:, cols], preferred_element_type=F32)
        acc = acc + _nt_dot(sin_both, ymat[cols, :])
        for h in range(w // t):
            y_ref[pn * (w // t) + h] = acc[:, h * t:(h + 1) * t]
        for q, pp in tiles[pn * per_panel:(pn + 1) * per_panel]:
            _s5_toep_tile(kw_scr.at[nslot], toep_scr.at[nslot], q, pp, wrap_mask, t=t, p=p)


def _s5_ssm(ut, nc, lam_re, lam_im, log_dt, b_re, b_im, c_re, c_im):
    e, bc, t = ut.shape
    _, g, n, p = b_re.shape
    n2 = 2 * n
    pt = p * t
    dup = lambda a: jnp.concatenate([a, a], axis=-1)
    lre = dup(lam_re)[:, :, None, :]
    lim = dup(lam_im)[:, :, None, :]
    ldt = log_dt[:, :, None, None]
    btre = dup(jnp.swapaxes(b_re, -1, -2))
    btim = dup(jnp.swapaxes(b_im, -1, -2))
    cre = dup(c_re)
    cim = dup(c_im)
    prm = (lre, lim, ldt, btre, btim, cre, cim)

    def this_group(a):
        return pl.BlockSpec((2, None) + a.shape[2:], lambda i: (0, i, 0, 0))

    def next_group(a):
        return pl.BlockSpec((2, None) + a.shape[2:], lambda i: (0, jnp.minimum(i + 1, g - 1), 0, 0))

    return pl.pallas_call(
        functools.partial(_s5_ssm_kernel, nc=nc, p=p, t=t, n=n),
        grid=(g,),
        in_specs=([pl.BlockSpec((p, bc, t), lambda i: (i, 0, 0))]
                  + [this_group(a) for a in prm] + [next_group(a) for a in prm]),
        out_specs=pl.BlockSpec((p, bc, t), lambda i: (i, 0, 0)),
        out_shape=_sds((e, bc, t), F32),
        scratch_shapes=[pltpu.VMEM((2, pt, pt), BF16),
                        pltpu.VMEM((2, pt, 2 * n2), BF16),
                        pltpu.VMEM((2, pt, 2 * n2), BF16),
                        pltpu.VMEM((2, 4, n2), F32),
                        pltpu.VMEM((2, 2, p * p, t), U32),
                        pltpu.VMEM((bc, pt), BF16),
                        pltpu.VMEM((2, bc, n2), F32),
                        pltpu.VMEM((2, bc, n2), F32),
                        pltpu.VMEM((2, bc, n2), F32)],
        compiler_params=_params("arbitrary"),
        name="s5_ssm",
    )(ut, *prm, *prm)


def _s5_post_kernel(y_ref, uzt_ref, x_ref, gate_ref, dcol_ref, bcol_ref, wgt_ref, wot_ref,
                    o_ref, rl_scr, *, e, t, sub, pitch):
    for ch in range(e):
        rl_scr[pl.ds(ch, V7X_SUBLANES, stride=pitch), :] = y_ref[ch]

    k = sub // t
    reps = sub // dcol_ref.shape[1]
    dfull = _cat([dcol_ref[...]] * reps, 1)
    bfull = _cat([bcol_ref[...]] * reps, 1)
    for s in range(x_ref.shape[1] // sub):
        tok = slice(s * sub, (s + 1) * sub)
        yt = _cat([rl_scr[(s * k + kk) * pitch:(s * k + kk) * pitch + e, :] for kk in range(k)], 1)
        ut = uzt_ref[0, :e, tok].astype(F32)
        y = jax.nn.gelu(yt + dfull * ut)
        g = jnp.dot(wgt_ref[...], y.astype(BF16), preferred_element_type=F32) + bfull
        y = y * _sigmoid(g)
        z = uzt_ref[0, e:, tok].astype(F32)
        yz = y * _silu(z)
        ot = jnp.dot(wot_ref[...], yz.astype(BF16), preferred_element_type=F32)
        o_ref[0, tok, :] = x_ref[0, tok, :] + gate_ref[0] * ot.T


def _s5_post(yt, uzt, x, gate, d_skip, b_glu, w_glu_t, w_out_t):
    b, l, d = x.shape
    e, bc, t = yt.shape
    tl = S5_TILE
    k = tl // t
    lanes = V7X_LANES
    dcol = jnp.broadcast_to(d_skip[:, None], (e, lanes))
    bcol = jnp.broadcast_to(b_glu[:, None], (e, lanes))
    steps = l // tl
    pitch = _relayout_pitch(e)
    return pl.pallas_call(
        functools.partial(_s5_post_kernel, e=e, t=t, sub=tl, pitch=pitch),
        grid=(b, steps),
        in_specs=[pl.BlockSpec((e, k, t), lambda i, j: (0, i * steps + j, 0)),
                  pl.BlockSpec((1, 2 * e, tl), lambda i, j: (i, 0, j)),
                  pl.BlockSpec((1, tl, d), lambda i, j: (i, j, 0)),
                  pl.BlockSpec((1, 1, d), lambda i, j: (i, 0, 0)),
                  pl.BlockSpec((e, lanes), lambda i, j: (0, 0)),
                  pl.BlockSpec((e, lanes), lambda i, j: (0, 0)),
                  pl.BlockSpec((e, e), lambda i, j: (0, 0)),
                  pl.BlockSpec((d, e), lambda i, j: (0, 0))],
        out_specs=pl.BlockSpec((1, tl, d), lambda i, j: (i, j, 0)),
        out_shape=_sds((b, l, d), F32),
        scratch_shapes=[pltpu.VMEM((k * pitch, t), F32)],
        compiler_params=_params("arbitrary", "arbitrary"),
        name="s5_post",
    )(yt, uzt, x, gate, dcol, bcol, w_glu_t, w_out_t)


def _s5_layer(x, shift, scale, gate, g, w_in, lam_re, lam_im, log_dt, b_re, b_im, c_re, c_im,
              d_skip, w_glu, b_glu, w_out):
    l = x.shape[1]
    assert l % S5_TILE == 0
    ut, uzt = _s5_pre(x, shift, scale, g, w_in.astype(BF16))
    yt = _s5_ssm(ut, l // S5_CHUNK, lam_re, lam_im, log_dt, b_re, b_im, c_re, c_im)
    return _s5_post(yt, uzt, x, gate, d_skip, b_glu, w_glu.T.astype(BF16), w_out.T.astype(BF16))


def _ml_pre_kernel(x_ref, shift_ref, scale_ref, g_ref, w_ref, xm_ref, z_ref, *, e):
    h = _modulated_norm(x_ref[0], g_ref[...], shift_ref[0], scale_ref[0])
    xz = jnp.dot(h.astype(BF16), w_ref[...], preferred_element_type=F32)
    xm_ref[0] = xz[:, :e].astype(BF16)
    z_ref[0] = xz[:, e:].astype(BF16)


def _ml_pre(x, shift, scale, g, w_in):
    b, l, d = x.shape
    e = w_in.shape[1] // 2
    tl = min(TOKEN_TILE, l)
    return pl.pallas_call(
        functools.partial(_ml_pre_kernel, e=e),
        grid=(b, l // tl),
        in_specs=[pl.BlockSpec((1, tl, d), lambda i, j: (i, j, 0)),
                  pl.BlockSpec((1, 1, d), lambda i, j: (i, 0, 0)),
                  pl.BlockSpec((1, 1, d), lambda i, j: (i, 0, 0)),
                  pl.BlockSpec((1, d), lambda i, j: (0, 0)),
                  pl.BlockSpec((d, 2 * e), lambda i, j: (0, 0))],
        out_specs=[pl.BlockSpec((1, tl, e), lambda i, j: (i, j, 0)),
                   pl.BlockSpec((1, tl, e), lambda i, j: (i, j, 0))],
        out_shape=[_sds((b, l, e), BF16), _sds((b, l, e), BF16)],
        compiler_params=_params("arbitrary", "arbitrary"),
        name="ml_pre",
    )(x, shift, scale, g, w_in)


HALO = 2 * V7X_SUBLANES


def _ml_qkv_kernel(xm_ref, prev_ref, next_ref, cw_ref, cb_ref, wq_ref, wk_ref, wv_ref, wg_ref,
                   bg_ref, q_ref, kt_ref, v_ref, xc_ref, gp_ref, *, steps, taps):
    j = pl.program_id(1)
    xm_b = xm_ref[0]
    tq, e = xm_b.shape
    zero = jnp.zeros((), BF16)
    prev = jnp.where(j == 0, zero, prev_ref[0])
    nxt = jnp.where(j == steps - 1, zero, next_ref[0])
    xe = jnp.concatenate([prev, xm_b, nxt], axis=0).astype(F32)
    pad = taps // 2
    acc = jnp.broadcast_to(cb_ref[...], (tq, e))
    for kk in range(taps):
        off = HALO + kk - pad
        acc = acc + xe[off:off + tq, :] * cw_ref[kk:kk + 1, :]
    xc = _silu(acc)
    xc_b = xc.astype(BF16)
    tile = wq_ref.shape[-1]
    qs, ks, vs = [], [], []
    for i in range(e // tile):
        cols = slice(i * tile, (i + 1) * tile)
        qs.append(jnp.dot(xc_b[:, cols], wq_ref[i], preferred_element_type=F32))
        ks.append(jnp.dot(xc_b[:, cols], wk_ref[i], preferred_element_type=F32).astype(BF16))
        vs.append(jnp.dot(xm_b[:, cols], wv_ref[i], preferred_element_type=F32))
    q = jnp.concatenate(qs, axis=1).astype(BF16)
    k = jnp.concatenate(ks, axis=1)
    v = jnp.concatenate(vs, axis=1).astype(BF16)
    gp = (jnp.dot(q, wg_ref[0], preferred_element_type=F32)
          + jnp.dot(k, wg_ref[1], preferred_element_type=F32)
          + jnp.dot(v, wg_ref[2], preferred_element_type=F32) + bg_ref[...])
    q_ref[0] = q
    for i in range(e // tile):
        kt_ref[0, i * tile:(i + 1) * tile, :] = ks[i].T
    v_ref[0] = v
    xc_ref[0] = xc_b
    gp_ref[0] = gp


def _blockdiag_tiles(w, tile):
    nblk, bs, _ = w.shape
    rows = w.reshape(nblk // (tile // bs), tile, bs)
    spread = jnp.tile(rows, (1, 1, tile // bs))
    blk = jnp.arange(tile) // bs
    return jnp.where(blk[:, None] == blk[None, :], spread, 0.0)


def _ml_qkv(xm, conv_w, conv_b, w_q, w_k, w_v, w_gates, b_gates):
    b, l, e = xm.shape
    tq = min(TOKEN_TILE, l)
    steps = l // tq
    taps = conv_w.shape[0]
    tile = V7X_MXU_DIM
    lanes = V7X_LANES
    ng = w_gates.shape[-1]
    wq = _blockdiag_tiles(w_q, tile).astype(BF16)
    wk = _blockdiag_tiles(w_k, tile).astype(BF16)
    wv = _blockdiag_tiles(w_v, tile).astype(BF16)
    wg = jnp.pad(w_gates, ((0, 0), (0, 0), (0, lanes - ng))).astype(BF16)
    bg = jnp.pad(b_gates, (0, lanes - ng)).reshape(1, lanes)
    hb = tq // HALO
    last = l // HALO - 1
    tok = lambda i, j: (i, j, 0)
    full3 = lambda i, j: (0, 0, 0)
    full2 = lambda i, j: (0, 0)
    return pl.pallas_call(
        functools.partial(_ml_qkv_kernel, steps=steps, taps=taps),
        grid=(b, steps),
        in_specs=[pl.BlockSpec((1, tq, e), tok),
                  pl.BlockSpec((1, HALO, e), lambda i, j: (i, jnp.maximum(j * hb - 1, 0), 0)),
                  pl.BlockSpec((1, HALO, e), lambda i, j: (i, jnp.minimum((j + 1) * hb, last), 0)),
                  pl.BlockSpec((taps, e), full2),
                  pl.BlockSpec((1, e), full2),
                  pl.BlockSpec(wq.shape, full3),
                  pl.BlockSpec(wk.shape, full3),
                  pl.BlockSpec(wv.shape, full3),
                  pl.BlockSpec(wg.shape, full3),
                  pl.BlockSpec((1, lanes), full2)],
        out_specs=[pl.BlockSpec((1, tq, e), tok), pl.BlockSpec((1, e, tq), lambda i, j: (i, 0, j)),
                   pl.BlockSpec((1, tq, e), tok), pl.BlockSpec((1, tq, e), tok),
                   pl.BlockSpec((1, tq, lanes), tok)],
        out_shape=[_sds((b, l, e), BF16), _sds((b, e, l), BF16), _sds((b, l, e), BF16),
                   _sds((b, l, e), BF16), _sds((b, l, lanes), F32)],
        compiler_params=_params("arbitrary", "arbitrary"),
        name="ml_qkv",
    )(xm, xm, xm, conv_w, conv_b.reshape(1, e), wq, wk, wv, wg, bg)


def _ml_gates_kernel(gp_ref, cb_ref, gg_ref, at_ref, gt_ref, *, nf, t):
    lanes = gp_ref.shape[2]
    ri = lax.broadcasted_iota(jnp.int32, (t, t), 0)
    ci = lax.broadcasted_iota(jnp.int32, (t, t), 1)
    lower = jnp.where(ci <= ri, 1.0, 0.0).astype(BF16)
    upper = jnp.where(ci >= ri, 1.0, 0.0).astype(BF16)

    def tri_sums(tri, parts):
        return sum(jnp.dot(tri, part, preferred_element_type=F32) for part in parts)

    fwd = lax.broadcasted_iota(jnp.int32, (t, lanes), 1) < 2 * nf
    fwd_row = lax.broadcasted_iota(jnp.int32, (1, lanes), 1) < 2 * nf
    for c in range(gp_ref.shape[1] // t):
        rows = slice(c * t, (c + 1) * t)
        x = gp_ref[0, rows, :]
        lf = jax.nn.log_sigmoid(x)
        ipre = pltpu.roll(x, nf, 1)
        hi = lf.astype(BF16)
        rest = lf - hi.astype(F32)
        mid = rest.astype(BF16)
        low = (rest - mid.astype(F32)).astype(BF16)
        cum_f = tri_sums(lower, (hi, mid, low))
        cum_b = tri_sums(upper, (hi, mid, low))
        cum = jnp.where(fwd, cum_f, cum_b)
        tot = jnp.where(fwd_row, cum_f[t - 1:t, :], cum_b[0:1, :])
        gg = tot - cum + ipre
        cb_ref[0, rows, :] = cum
        gg_ref[0, rows, :] = gg
        at_ref[0, c] = (ipre - cum).T
        gt_ref[0, c] = gg.T


def _ml_gates(gp, nheads):
    b, l, lanes = gp.shape
    t = ML_CHUNK
    nc = l // t
    return pl.pallas_call(
        functools.partial(_ml_gates_kernel, nf=nheads, t=t),
        grid=(b,),
        in_specs=[pl.BlockSpec((1, l, lanes), lambda i: (i, 0, 0))],
        out_specs=[pl.BlockSpec((1, l, lanes), lambda i: (i, 0, 0)),
                   pl.BlockSpec((1, l, lanes), lambda i: (i, 0, 0)),
                   pl.BlockSpec((1, nc, lanes, t), lambda i: (i, 0, 0, 0)),
                   pl.BlockSpec((1, nc, lanes, t), lambda i: (i, 0, 0, 0))],
        out_shape=[_sds((b, l, lanes), F32), _sds((b, l, lanes), F32), _sds((b, nc, lanes, t), F32),
                   _sds((b, nc, lanes, t), F32)],
        compiler_params=_params("arbitrary"),
        name="ml_gates",
    )(gp)


def _ml_main_kernel(q_ref, kt_ref, v_ref, cb_ref, gg_ref, at_ref, gt_ref, xc_ref, z_ref, x_ref, gate_ref,
                    gnw_ref, skip_ref, wout_ref, *rest, nc, nh, final):
    if final:
        fg_ref, o_ref, c_scr, n_scr, m_scr, hf_scr = rest
    else:
        o_ref, c_scr, n_scr, m_scr, hf_scr = rest
    dirn = pl.program_id(1)
    step = pl.program_id(2)
    t = q_ref.shape[1]
    dh = q_ref.shape[2] // nh
    scale = float(dh) ** -0.5

    @pl.when(step == 0)
    def _():
        c_scr[...] = jnp.zeros_like(c_scr)
        n_scr[...] = jnp.zeros_like(n_scr)
        m_scr[...] = jnp.zeros_like(m_scr)

    ri = lax.broadcasted_iota(jnp.int32, (t, t), 0)
    ci = lax.broadcasted_iota(jnp.int32, (t, t), 1)
    ones_rows = jnp.ones((V7X_SUBLANES, t), BF16)

    def heads_out(d):
        keep = (ci <= ri) if d == 0 else (ci >= ri)
        stage = []
        for h in range(nh):
            cols = slice(h * dh, (h + 1) * dh)
            gcol = d * 2 * nh + nh + h
            q = q_ref[0, :, cols]
            kt = kt_ref[0, cols, :]
            cb = cb_ref[0, :, gcol:gcol + 1]
            gg = gg_ref[0, :, gcol:gcol + 1]
            a_row = at_ref[0, 0, gcol:gcol + 1, :]
            g_row = gt_ref[0, 0, gcol:gcol + 1, :]
            tot = cb[t - 1:t, :] if d == 0 else cb[0:1, :]
            m = m_scr[h][:, 0:1]
            dmat = jnp.where(keep, cb + a_row, -jnp.inf)
            inter = cb + m
            m_t = jnp.maximum(inter, jnp.max(dmat, axis=1, keepdims=True))
            m_new = jnp.maximum(tot + m, jnp.max(gg, axis=0, keepdims=True))
            stage.append(dict(
                q=q, kt=kt, m_t=m_t, m_new=m_new,
                w_intra=jnp.exp(dmat - m_t),
                w_inter=jnp.exp(inter - m_t),
                qk=jnp.dot(q, kt, preferred_element_type=F32),
                decay=jnp.exp(tot + m - m_new),
                wk_row=jnp.exp(g_row - m_new) * scale))
        for h in range(nh):
            cols = slice(h * dh, (h + 1) * dh)
            a = stage[h]
            v = v_ref[0, :, cols]
            s = a['qk'] * scale * a['w_intra']
            qw = a['q'] * a['w_inter'].astype(BF16)
            num = jnp.dot(s.astype(BF16), v, preferred_element_type=F32)
            a['num'] = num + jnp.dot(qw, c_scr[h].astype(BF16), preferred_element_type=F32)
            qn = jnp.sum(a['q'].astype(F32) * n_scr[h], axis=1, keepdims=True)
            a['nq'] = jnp.sum(s, axis=1, keepdims=True) + a['w_inter'] * qn
        for h in range(nh):
            cols = slice(h * dh, (h + 1) * dh)
            a = stage[h]
            v = v_ref[0, :, cols]
            a['hout'] = a['num'] / jnp.maximum(jnp.abs(a['nq']), jnp.exp(-a['m_t']))
            kwt = a['kt'] * a['wk_row'].astype(BF16)
            a['c_new'] = a['decay'] * c_scr[h] + jnp.dot(kwt, v, preferred_element_type=F32)
            a['n_add'] = _nt_dot(ones_rows, kwt)[0:1, :]
        for h in range(nh):
            a = stage[h]
            c_scr[h] = a['c_new']
            n_scr[h] = a['decay'] * n_scr[h] + a['n_add']
            m_scr[h] = jnp.broadcast_to(a['m_new'], m_scr.shape[1:])
        return [a['hout'] for a in stage]

    @pl.when(dirn == 0)
    def _():
        rows = pl.ds(pl.multiple_of(step * t, t), t)
        for h, hout in enumerate(heads_out(0)):
            hf_scr[rows, h * dh:(h + 1) * dh] = hout

    @pl.when(dirn == 1)
    def _():
        rows = pl.ds(pl.multiple_of((nc - 1 - step) * t, t), t)
        acc = None
        for h, hout in enumerate(heads_out(1)):
            cols = slice(h * dh, (h + 1) * dh)
            hs = hf_scr[rows, cols] + hout
            mu = jnp.mean(hs, axis=1, keepdims=True)
            cen = hs - mu
            var = jnp.mean(cen * cen, axis=1, keepdims=True)
            hn = cen * lax.rsqrt(var + LN_EPS)
            y = hn * gnw_ref[:, cols] + skip_ref[:, cols] * xc_ref[0, :, cols].astype(F32)
            y = y * _silu(z_ref[0, :, cols].astype(F32))
            part = jnp.dot(y.astype(BF16), wout_ref[cols, :], preferred_element_type=F32)
            acc = part if acc is None else acc + part
        out = x_ref[0] + gate_ref[0] * acc
        o_ref[0] = _rms(out, fg_ref[...]) if final else out


def _ml_main(q, kt, v, cb, gg, at, gt, xc, z, x, gate, gn_w, skip, w_out, nheads, final_g):
    b, l, e = q.shape
    d = x.shape[-1]
    t = ML_CHUNK
    nc = l // t
    lanes = cb.shape[-1]
    dh = e // nheads
    chunk = lambda i, dd, s: (i, s + dd * (nc - 1 - 2 * s), 0)
    chunk_t = lambda i, dd, s: (i, 0, s + dd * (nc - 1 - 2 * s))
    chunk4 = lambda i, dd, s: (i, s + dd * (nc - 1 - 2 * s), 0, 0)
    late = lambda i, dd, s: (i, (nc - 1) - dd * s, 0)
    vec = lambda i, dd, s: (0, 0)
    final = final_g is not None
    extra_specs = [pl.BlockSpec((1, d), vec)] if final else []
    extra_args = [final_g.reshape(1, d)] if final else []
    return pl.pallas_call(
        functools.partial(_ml_main_kernel, nc=nc, nh=nheads, final=final),
        grid=(b, 2, nc),
        in_specs=[pl.BlockSpec((1, t, e), chunk),
                  pl.BlockSpec((1, e, t), chunk_t),
                  pl.BlockSpec((1, t, e), chunk),
                  pl.BlockSpec((1, t, lanes), chunk),
                  pl.BlockSpec((1, t, lanes), chunk),
                  pl.BlockSpec((1, 1, lanes, t), chunk4),
                  pl.BlockSpec((1, 1, lanes, t), chunk4),
                  pl.BlockSpec((1, t, e), late),
                  pl.BlockSpec((1, t, e), late),
                  pl.BlockSpec((1, t, d), late),
                  pl.BlockSpec((1, 1, d), lambda i, dd, s: (i, 0, 0)),
                  pl.BlockSpec((1, e), vec),
                  pl.BlockSpec((1, e), vec),
                  pl.BlockSpec((e, d), vec)] + extra_specs,
        out_specs=pl.BlockSpec((1, t, d), late),
        out_shape=_sds((b, l, d), F32),
        scratch_shapes=[pltpu.VMEM((nheads, dh, dh), F32),
                        pltpu.VMEM((nheads, 1, dh), F32),
                        pltpu.VMEM((nheads, 1, V7X_LANES), F32),
                        pltpu.VMEM((l, e), F32)],
        compiler_params=_params("arbitrary", "arbitrary", "arbitrary"),
        name="ml_main",
    )(q, kt, v, cb, gg, at, gt, xc, z, x, gate, gn_w.reshape(1, e), skip.reshape(1, e), w_out.astype(BF16),
      *extra_args)


def _ml_layer(x, shift, scale, gate, g, w_in, conv_w, conv_b, w_q, w_k, w_v, w_gates, b_gates,
              gn_w, skip, w_out, final_g):
    assert x.shape[1] % ML_CHUNK == 0
    nheads = b_gates.shape[0] // 4
    xm, z = _ml_pre(x, shift, scale, g, w_in.astype(BF16))
    q, kt, v, xc, gp = _ml_qkv(xm, conv_w, conv_b, w_q, w_k, w_v, w_gates, b_gates)
    cb, gg, at, gt = _ml_gates(gp, nheads)
    return _ml_main(q, kt, v, cb, gg, at, gt, xc, z, x, gate, gn_w, skip, w_out, nheads, final_g)


def _final_norm_kernel(x_ref, g_ref, o_ref):
    o_ref[0] = _rms(x_ref[0], g_ref[...])


def _final_norm(x, g):
    b, l, d = x.shape
    tl = min(TOKEN_TILE, l)
    return pl.pallas_call(
        _final_norm_kernel,
        grid=(b, l // tl),
        in_specs=[pl.BlockSpec((1, tl, d), lambda i, j: (i, j, 0)),
                  pl.BlockSpec((1, d), lambda i, j: (0, 0))],
        out_specs=pl.BlockSpec((1, tl, d), lambda i, j: (i, j, 0)),
        out_shape=_sds((b, l, d), F32),
        compiler_params=_params("arbitrary", "arbitrary"),
        name="final_norm",
    )(x, g.reshape(1, d))


def kernel(x, c, ada_w, ada_b, norm_g, s5_w_in, s5_lam_re, s5_lam_im, s5_log_dt, s5_b_re, s5_b_im, s5_c_re, s5_c_im, s5_d, s5_w_glu, s5_b_glu, s5_w_out, ml_w_in, ml_conv_w, ml_conv_b, ml_w_q, ml_w_k, ml_w_v, ml_w_gates, ml_b_gates, ml_gn_w, ml_skip, ml_w_out, final_g):
    depth = ada_w.shape[0]
    d = x.shape[-1]
    mod = _ada(c, ada_w, ada_b)
    for i in range(depth):
        shift = mod[i, :, None, 0:d]
        scale = mod[i, :, None, d:2 * d]
        gate = mod[i, :, None, 2 * d:3 * d]
        g = norm_g[i].reshape(1, d)
        j = i // 2
        if i % 2 == 0:
            x = _s5_layer(x, shift, scale, gate, g, s5_w_in[j], s5_lam_re[j], s5_lam_im[j], s5_log_dt[j],
                          s5_b_re[j], s5_b_im[j], s5_c_re[j], s5_c_im[j], s5_d[j], s5_w_glu[j],
                          s5_b_glu[j], s5_w_out[j])
        else:
            x = _ml_layer(x, shift, scale, gate, g, ml_w_in[j], ml_conv_w[j], ml_conv_b[j], ml_w_q[j],
                          ml_w_k[j], ml_w_v[j], ml_w_gates[j], ml_b_gates[j], ml_gn_w[j], ml_skip[j],
                          ml_w_out[j], final_g if i == depth - 1 else None)
    return x if depth % 2 == 0 else _final_norm(x, final_g)
```
